```python
import jax, jax.numpy as jnp
from jax import lax
import numpy as np

D_MODEL = 1024
BATCH = 32
SEQ = 2048
DEPTH = 4
DEC_BATCH = 4
DEC_SEQ = 4096
PAST_LEN = 128

N_HEADS = 8
QK_NOPE = 128
QK_ROPE = 64
V_HEAD = 128
Q_LORA = 512
KV_LORA = 256
ATTN_WIDTH = N_HEADS * V_HEAD
QK_HEAD = QK_NOPE + QK_ROPE
ROPE_BASE = 10000.0
Q_BLOCK = 128
N_FOURIER_GROUPS = 4
FOURIER_GROUP = 128
FOURIER_WIDTH = N_FOURIER_GROUPS * FOURIER_GROUP
D_FF = 2816
CONV_WIDTH = 3
EPS = 1e-6
N_ADA = 6
OFF_Q = 0
OFF_KV = OFF_Q + Q_LORA
OFF_KR = OFF_KV + KV_LORA
OFF_F = OFF_KR + QK_ROPE
OFF_G = OFF_F + FOURIER_WIDTH
IN_WIDTH = OFF_G + 2 * D_MODEL

kernel_name = 'hybrid_mla_fnet_convffn_encoder'


def rms_norm(x, g):
    xf = x.astype(jnp.float32)
    y = xf * lax.rsqrt(jnp.mean(xf * xf, axis=-1, keepdims=True) + EPS)
    return (y * g.astype(jnp.float32)).astype(x.dtype)


def rope_tables(seq_len):
    half = QK_ROPE // 2
    inv = 1.0 / (ROPE_BASE ** (jnp.arange(half, dtype=jnp.float32) / half))
    ang = jnp.arange(seq_len, dtype=jnp.float32)[:, None] * inv[None, :]
    return jnp.cos(ang), jnp.sin(ang)


def apply_rope(x, cos, sin):
    x1, x2 = jnp.split(x.astype(jnp.float32), 2, axis=-1)
    c = cos[None, :, None, :]
    s = sin[None, :, None, :]
    return jnp.concatenate([x1 * c - x2 * s, x2 * c + x1 * s], axis=-1).astype(x.dtype)


def bidir_attention(q, k, v):
    B, S, H, Dq = q.shape
    nb = S // Q_BLOCK
    scale = Dq ** -0.5
    qb = q.reshape(B, nb, Q_BLOCK, H, Dq).transpose(1, 0, 2, 3, 4)

    def block(qi):
        s = jnp.einsum('bqhd,bkhd->bhqk', qi, k).astype(jnp.float32) * scale
        p = jax.nn.softmax(s, axis=-1).astype(v.dtype)
        return jnp.einsum('bhqk,bkhd->bqhd', p, v)

    o = lax.map(block, qb)
    return o.transpose(1, 0, 2, 3, 4).reshape(B, S, H * v.shape[-1])


def fourier_mix(u):
    B, S, _ = u.shape
    ug = u.reshape(B, S, N_FOURIER_GROUPS, FOURIER_GROUP).astype(jnp.float32)
    f = jnp.fft.fft2(ug, axes=(1, 3), norm='ortho').real
    return f.reshape(B, S, FOURIER_WIDTH).astype(u.dtype)


def centred_dwconv(h, w, b):
    S = h.shape[1]
    hp = jnp.pad(h, ((0, 0), (1, 1), (0, 0)))
    return hp[:, :S] * w[0] + hp[:, 1:S + 1] * w[1] + hp[:, 2:] * w[2] + b


def mixer_block(h, cos, sin, w_in, g_q, w_q_b, g_kv, w_kv_b, w_attn_o, w_four, w_out):
    B, S, _ = h.shape
    z = h @ w_in
    c_q = rms_norm(z[..., OFF_Q:OFF_KV], g_q)
    c_kv = rms_norm(z[..., OFF_KV:OFF_KR], g_kv)
    k_rope = z[..., OFF_KR:OFF_F].reshape(B, S, 1, QK_ROPE)
    u_f = z[..., OFF_F:OFF_G]
    gate_a, gate_b = jnp.split(z[..., OFF_G:], 2, axis=-1)

    q = (c_q @ w_q_b).reshape(B, S, N_HEADS, QK_HEAD)
    q = jnp.concatenate([q[..., :QK_NOPE], apply_rope(q[..., QK_NOPE:], cos, sin)], axis=-1)
    kv = (c_kv @ w_kv_b).reshape(B, S, N_HEADS, QK_NOPE + V_HEAD)
    k_nope, v = kv[..., :QK_NOPE], kv[..., QK_NOPE:]
    k_rope = jnp.broadcast_to(apply_rope(k_rope, cos, sin), (B, S, N_HEADS, QK_ROPE))
    k = jnp.concatenate([k_nope, k_rope], axis=-1)

    branch_a = bidir_attention(q, k, v) @ w_attn_o
    branch_b = fourier_mix(u_f) @ w_four
    merged = jax.nn.sigmoid(gate_a) * branch_a + jax.nn.sigmoid(gate_b) * branch_b
    return merged @ w_out


def conv_ffn(h, w_up, w_conv, b_conv, w_down):
    up = centred_dwconv(h @ w_up, w_conv, b_conv)
    a, b = jnp.split(up, 2, axis=-1)
    return (jax.nn.gelu(a, approximate=True) * b) @ w_down


def trunk(x, c, w_ada, b_ada, g_mix_pre, g_mix_post, w_in, g_q, w_q_b, g_kv, w_kv_b,
          w_attn_o, w_four, w_out, g_ffn_pre, g_ffn_post, w_up, w_conv, b_conv, w_down):
    cos, sin = rope_tables(x.shape[1])
    for l in range(DEPTH):
        mod = jax.nn.silu(c) @ w_ada[l] + b_ada[l]
        sh1, sc1, gt1, sh2, sc2, gt2 = [m[:, None, :] for m in jnp.split(mod, N_ADA, axis=-1)]
        h = rms_norm(x, g_mix_pre[l]) * (1 + sc1) + sh1
        y = mixer_block(h, cos, sin, w_in[l], g_q[l], w_q_b[l], g_kv[l], w_kv_b[l],
                        w_attn_o[l], w_four[l], w_out[l])
        x = x + gt1 * rms_norm(y, g_mix_post[l])
        h = rms_norm(x, g_ffn_pre[l]) * (1 + sc2) + sh2
        y = conv_ffn(h, w_up[l], w_conv[l], b_conv[l], w_down[l])
        x = x + gt2 * rms_norm(y, g_ffn_post[l])
    return x


def setup_inputs(seed: int = 0) -> dict:
    key = jax.random.key(seed)
    ks = jax.random.split(key, 24)
    f32 = jnp.float32

    def dense(k, fan_in, fan_out, scale=1.0):
        return jax.random.normal(k, (DEPTH, fan_in, fan_out), f32) * (scale * fan_in ** -0.5)

    def gain(k, dim):
        return 1.0 + 0.02 * jax.random.normal(k, (DEPTH, dim), f32)

    return {
        'x_prompt': jax.random.normal(ks[0], (BATCH, SEQ, D_MODEL), f32),
        'x_sample': jax.random.normal(ks[1], (DEC_BATCH, DEC_SEQ, D_MODEL), f32),
        'c_prompt': jax.random.normal(ks[2], (BATCH, D_MODEL), f32),
        'c_sample': jax.random.normal(ks[3], (DEC_BATCH, D_MODEL), f32),
        'w_ada': dense(ks[4], D_MODEL, N_ADA * D_MODEL, 0.5),
        'b_ada': 0.02 * jax.random.normal(ks[5], (DEPTH, N_ADA * D_MODEL), f32),
        'g_mix_pre': gain(ks[6], D_MODEL),
        'g_mix_post': gain(ks[7], D_MODEL),
        'w_in': dense(ks[8], D_MODEL, IN_WIDTH),
        'g_q': gain(ks[9], Q_LORA),
        'w_q_b': dense(ks[10], Q_LORA, N_HEADS * QK_HEAD),
        'g_kv': gain(ks[11], KV_LORA),
        'w_kv_b': dense(ks[12], KV_LORA, N_HEADS * (QK_NOPE + V_HEAD)),
        'w_attn_o': dense(ks[13], ATTN_WIDTH, D_MODEL),
        'w_four': dense(ks[14], FOURIER_WIDTH, D_MODEL),
        'w_out': dense(ks[15], D_MODEL, D_MODEL),
        'g_ffn_pre': gain(ks[16], D_MODEL),
        'g_ffn_post': gain(ks[17], D_MODEL),
        'w_up': dense(ks[18], D_MODEL, 2 * D_FF),
        'w_conv': jax.random.normal(ks[19], (DEPTH, CONV_WIDTH, 2 * D_FF), f32) * (CONV_WIDTH ** -0.5),
        'b_conv': 0.02 * jax.random.normal(ks[20], (DEPTH, 2 * D_FF), f32),
        'w_down': dense(ks[21], D_FF, D_MODEL),
    }


def reference(x_prompt, x_sample, c_prompt, c_sample, w_ada, b_ada, g_mix_pre, g_mix_post,
              w_in, g_q, w_q_b, g_kv, w_kv_b, w_attn_o, w_four, w_out, g_ffn_pre, g_ffn_post,
              w_up, w_conv, b_conv, w_down):
    y_prompt = trunk(x_prompt, c_prompt, w_ada, b_ada, g_mix_pre, g_mix_post, w_in, g_q, w_q_b,
                     g_kv, w_kv_b, w_attn_o, w_four, w_out, g_ffn_pre, g_ffn_post,
                     w_up, w_conv, b_conv, w_down)
    y_sample = trunk(x_sample, c_sample, w_ada, b_ada, g_mix_pre, g_mix_post, w_in, g_q, w_q_b,
                     g_kv, w_kv_b, w_attn_o, w_four, w_out, g_ffn_pre, g_ffn_post,
                     w_up, w_conv, b_conv, w_down)
    return (y_prompt, y_sample)
```

```python
import functools
import math

import jax
import jax.numpy as jnp
from jax import lax
from jax.experimental import pallas as pl
from jax.experimental.pallas import tpu as pltpu

D_MODEL = 1024
DEPTH = 4
N_HEADS = 8
QK_NOPE = 128
QK_ROPE = 64
V_HEAD = 128
Q_LORA = 512
KV_LORA = 256
QK_HEAD = QK_NOPE + QK_ROPE
ROPE_BASE = 10000.0
N_FOURIER_GROUPS = 4
FOURIER_GROUP = 128
FOURIER_WIDTH = N_FOURIER_GROUPS * FOURIER_GROUP
D_FF = 2816
EPS = 1e-6
N_ADA = 6
OFF_KV = Q_LORA
OFF_KR = OFF_KV + KV_LORA
OFF_F = OFF_KR + QK_ROPE
OFF_G = OFF_F + FOURIER_WIDTH

LANES = 128
SUBLANES = 8
QK_PAD = 2 * LANES
VMEM_LIMIT_BYTES = 56 * 1024 * 1024

BF16 = jnp.bfloat16
F32 = jnp.float32


def _dot(a, b):
    return jnp.dot(a, b, preferred_element_type=F32)


def _dot_nt(a, b):
    return lax.dot_general(a, b, (((1,), (1,)), ((), ())), preferred_element_type=F32)


def _rms(x, g):
    return x * lax.rsqrt(jnp.mean(x * x, axis=-1, keepdims=True) + EPS) * g


def _const_spec(shape):
    nd = len(shape)
    return pl.BlockSpec(shape, lambda *_: (0,) * nd, pipeline_mode=pl.Buffered(1))


def _params(*sem):
    return pltpu.CompilerParams(dimension_semantics=sem, vmem_limit_bytes=VMEM_LIMIT_BYTES)


def _ada_kernel(c_ref, w_ref, b_ref, o_ref):
    c = c_ref[...]
    s = c * jax.nn.sigmoid(c)
    o_ref[0] = jnp.dot(s, w_ref[0], preferred_element_type=F32,
                       precision=lax.Precision.HIGHEST) + b_ref[0]


def _ada_mods(c, w_ada, b_ada):
    b = c.shape[0]
    bp = -(-b // SUBLANES) * SUBLANES
    cp = jnp.pad(c, ((0, bp - b), (0, 0)))
    out = pl.pallas_call(
        _ada_kernel,
        grid=(DEPTH, N_ADA),
        in_specs=[
            pl.BlockSpec((bp, D_MODEL), lambda l, j: (0, 0)),
            pl.BlockSpec((1, D_MODEL, D_MODEL), lambda l, j: (l, 0, j)),
            pl.BlockSpec((1, 1, D_MODEL), lambda l, j: (l, 0, j)),
        ],
        out_specs=pl.BlockSpec((1, bp, D_MODEL), lambda l, j: (l, 0, j)),
        out_shape=jax.ShapeDtypeStruct((DEPTH, bp, N_ADA * D_MODEL), F32),
        compiler_params=_params("arbitrary", "arbitrary"),
        name="ada",
    )(cp, w_ada, b_ada.reshape(DEPTH, 1, N_ADA * D_MODEL))
    return out[:, :b]


def _pre_kernel(x_ref, sh_ref, sc_ref, gpre_ref, wqa_ref, wkva_ref, wkr_ref, wf_ref, wg_ref,
                gq_ref, gkv_ref, wqn_ref, wqr_ref, wkb_ref, wvbt_ref, cos_ref, sin_ref,
                q_ref, k_ref, vt_ref, u_ref, g_ref):
    tm = x_ref.shape[1]
    x = x_ref[0]
    h = _rms(x, gpre_ref[...]) * (1.0 + sc_ref[0]) + sh_ref[0]
    hb = h.astype(BF16)

    u_ref[0] = _dot(hb, wf_ref[...]).astype(BF16)
    g_ref[0] = _dot(hb, wg_ref[...]).astype(BF16)
    cq = _rms(_dot(hb, wqa_ref[...]), gq_ref[...]).astype(BF16)
    ckv = _rms(_dot(hb, wkva_ref[...]), gkv_ref[...]).astype(BF16)

    cos = cos_ref[...]
    sin = sin_ref[...]
    lane = lax.broadcasted_iota(jnp.int32, (tm, LANES), 1)
    low_half = (lane % QK_ROPE) < (QK_ROPE // 2)

    def rope(t):
        rot = jnp.where(low_half, pltpu.roll(t, LANES - QK_ROPE // 2, 1), pltpu.roll(t, QK_ROPE // 2, 1))
        return t * cos + rot * sin

    scale = QK_HEAD ** -0.5
    kr = rope(_dot(hb, wkr_ref[...])).astype(BF16)
    qn = _dot(cq, wqn_ref[...])
    qr = _dot(cq, wqr_ref[...])
    kn = _dot(ckv, wkb_ref[...])
    for hd in range(N_HEADS):
        lo = hd * QK_PAD
        sl = slice(hd * LANES, (hd + 1) * LANES)
        q_ref[0, :, lo:lo + LANES] = (qn[:, sl] * scale).astype(BF16)
        q_ref[0, :, lo + LANES:lo + QK_PAD] = (rope(qr[:, sl]) * scale).astype(BF16)
        k_ref[0, :, lo:lo + LANES] = kn[:, sl].astype(BF16)
        k_ref[0, :, lo + LANES:lo + QK_PAD] = kr
    vt_ref[0] = _dot_nt(wvbt_ref[...], ckv).astype(BF16)


def _pre_call(x, sh, sc, lw, cos_t, sin_t, tm):
    b, s, _ = x.shape
    row = lambda bi, i: (bi, i, 0)
    vec = lambda bi, i: (bi, 0, 0)
    weights = [lw["g_mix_pre"], lw["w_qa"], lw["w_kva"], lw["w_kr"], lw["w_f"], lw["w_g"],
               lw["g_q"], lw["g_kv"], lw["w_qn"], lw["w_qr"], lw["w_kb"], lw["w_vbt"]]
    return pl.pallas_call(
        _pre_kernel,
        grid=(b, s // tm),
        in_specs=[pl.BlockSpec((1, tm, D_MODEL), row),
                  pl.BlockSpec((1, 1, D_MODEL), vec),
                  pl.BlockSpec((1, 1, D_MODEL), vec)]
                 + [_const_spec(w.shape) for w in weights]
                 + [pl.BlockSpec((tm, LANES), lambda bi, i: (i, 0)),
                    pl.BlockSpec((tm, LANES), lambda bi, i: (i, 0))],
        out_specs=[pl.BlockSpec((1, tm, N_HEADS * QK_PAD), row),
                   pl.BlockSpec((1, tm, N_HEADS * QK_PAD), row),
                   pl.BlockSpec((1, N_HEADS * V_HEAD, tm), lambda bi, i: (bi, 0, i)),
                   pl.BlockSpec((1, tm, FOURIER_WIDTH), row),
                   pl.BlockSpec((1, tm, 2 * D_MODEL), row)],
        out_shape=[jax.ShapeDtypeStruct((b, s, N_HEADS * QK_PAD), BF16),
                   jax.ShapeDtypeStruct((b, s, N_HEADS * QK_PAD), BF16),
                   jax.ShapeDtypeStruct((b, N_HEADS * V_HEAD, s), BF16),
                   jax.ShapeDtypeStruct((b, s, FOURIER_WIDTH), BF16),
                   jax.ShapeDtypeStruct((b, s, 2 * D_MODEL), BF16)],
        compiler_params=_params("parallel", "parallel"),
        name="pre",
    )(x, sh, sc, *weights, cos_t, sin_t)


def _four_kernel(u_ref, wcs_ref, ccat_ref, o_ref, ab_ref):
    s = u_ref.shape[1]

    @pl.when(pl.program_id(1) == 0)
    def _():
        for g in range(N_FOURIER_GROUPS):
            sl = slice(g * FOURIER_GROUP, (g + 1) * FOURIER_GROUP)
            ab = _dot(u_ref[0, :, sl], wcs_ref[...])
            ab_ref[0:s, sl] = ab[:, :FOURIER_GROUP].astype(BF16)
            ab_ref[s:2 * s, sl] = ab[:, FOURIER_GROUP:].astype(BF16)

    o_ref[0] = _dot(ccat_ref[...], ab_ref[...]).astype(BF16)


def _four_call(u, wcs, ccat, tr):
    b, s, _ = u.shape
    return pl.pallas_call(
        _four_kernel,
        grid=(b, s // tr),
        in_specs=[pl.BlockSpec((1, s, FOURIER_WIDTH), lambda bi, r: (bi, 0, 0)),
                  _const_spec(wcs.shape),
                  pl.BlockSpec((tr, 2 * s), lambda bi, r: (r, 0))],
        out_specs=pl.BlockSpec((1, tr, FOURIER_WIDTH), lambda bi, r: (bi, r, 0)),
        out_shape=jax.ShapeDtypeStruct((b, s, FOURIER_WIDTH), BF16),
        scratch_shapes=[pltpu.VMEM((2 * s, FOURIER_WIDTH), BF16)],
        compiler_params=_params("parallel", "arbitrary"),
        name="four",
    )(u, wcs, ccat)


def _attn_kernel(q_ref, k_ref, vt_ref, o_ref):
    st = _dot_nt(k_ref[0], q_ref[0])
    m = jnp.max(st, axis=0, keepdims=True)
    p = jnp.exp(st - m)
    l = jnp.sum(p, axis=0, keepdims=True)
    ot = _dot(vt_ref[0], p.astype(BF16))
    o_ref[0] = (ot / l).T.astype(BF16)


def _attn_call(q, k, vt, tq):
    b, s, _ = q.shape
    return pl.pallas_call(
        _attn_kernel,
        grid=(b, N_HEADS, s // tq),
        in_specs=[pl.BlockSpec((1, tq, QK_PAD), lambda bi, h, i: (bi, i, h)),
                  pl.BlockSpec((1, s, QK_PAD), lambda bi, h, i: (bi, 0, h)),
                  pl.BlockSpec((1, V_HEAD, s), lambda bi, h, i: (bi, h, 0))],
        out_specs=pl.BlockSpec((1, tq, V_HEAD), lambda bi, h, i: (bi, i, h)),
        out_shape=jax.ShapeDtypeStruct((b, s, N_HEADS * V_HEAD), BF16),
        compiler_params=_params("parallel", "parallel", "parallel"),
        name="attn",
    )(q, k, vt)


def _merge_kernel(a_ref, f_ref, g_ref, x_ref, gt_ref, gpost_ref, wao_ref, wfo_ref, wout_ref, o_ref):
    branch_a = _dot(a_ref[0], wao_ref[...])
    branch_b = _dot(f_ref[0], wfo_ref[...])
    gate_a = g_ref[0, :, :D_MODEL].astype(F32)
    gate_b = g_ref[0, :, D_MODEL:].astype(F32)
    merged = jax.nn.sigmoid(gate_a) * branch_a + jax.nn.sigmoid(gate_b) * branch_b
    y = _dot(merged.astype(BF16), wout_ref[...])
    o_ref[0] = x_ref[0] + gt_ref[0] * _rms(y, gpost_ref[...])


def _merge_call(attn, four, gates, x, gt, lw, tm):
    b, s, _ = x.shape
    row = lambda bi, i: (bi, i, 0)
    weights = [lw["g_mix_post"], lw["w_attn_o"], lw["w_four"], lw["w_out"]]
    return pl.pallas_call(
        _merge_kernel,
        grid=(b, s // tm),
        in_specs=[pl.BlockSpec((1, tm, N_HEADS * V_HEAD), row),
                  pl.BlockSpec((1, tm, FOURIER_WIDTH), row),
                  pl.BlockSpec((1, tm, 2 * D_MODEL), row),
                  pl.BlockSpec((1, tm, D_MODEL), row),
                  pl.BlockSpec((1, 1, D_MODEL), lambda bi, i: (bi, 0, 0))]
                 + [_const_spec(w.shape) for w in weights],
        out_specs=pl.BlockSpec((1, tm, D_MODEL), row),
        out_shape=jax.ShapeDtypeStruct(x.shape, F32),
        compiler_params=_params("parallel", "parallel"),
        name="merge",
    )(attn, four, gates, x, gt, *weights)


FFN_CHUNK = 2 * LANES
assert D_FF % FFN_CHUNK == 0
GELU_C = math.sqrt(2.0 / math.pi)


def _gelu_tanh(a):
    return 0.5 * a * (1.0 + jnp.tanh(GELU_C * (a + 0.044715 * (a * a * a))))


def _ffn_kernel(xp_ref, x_ref, xn_ref, sh_ref, sc_ref, gt_ref, gpre_ref, gpost_ref,
                wua_ref, wub_ref, wca_ref, wcb_ref, bca_ref, bcb_ref, wd_ref, o_ref, act_ref):
    i = pl.program_id(1)
    last = pl.num_programs(1) - 1
    tm = x_ref.shape[1]
    te = tm + 2 * SUBLANES
    x = x_ref[0]
    xe = jnp.concatenate([xp_ref[0], x, xn_ref[0]], axis=0)
    he = _rms(xe, gpre_ref[...]) * (1.0 + sc_ref[0]) + sh_ref[0]
    r = lax.broadcasted_iota(jnp.int32, (te, 1), 0)
    inside = jnp.logical_and(jnp.logical_or(r >= SUBLANES, i > 0),
                             jnp.logical_or(r < tm + SUBLANES, i < last))
    he = jnp.where(inside, he, 0.0).astype(BF16)

    def conv(u, w_ref, b_ref, cs):
        prev = pltpu.roll(u, 1, 0)[SUBLANES:SUBLANES + tm]
        nxt = pltpu.roll(u, te - 1, 0)[SUBLANES:SUBLANES + tm]
        mid = u[SUBLANES:SUBLANES + tm]
        return prev * w_ref[0:1, cs] + mid * w_ref[1:2, cs] + nxt * w_ref[2:3, cs] + b_ref[:, cs]

    for c in range(D_FF // FFN_CHUNK):
        cs = slice(c * FFN_CHUNK, (c + 1) * FFN_CHUNK)
        a = conv(_dot(he, wua_ref[:, cs]), wca_ref, bca_ref, cs)
        g = conv(_dot(he, wub_ref[:, cs]), wcb_ref, bcb_ref, cs)
        act_ref[:, cs] = (_gelu_tanh(a) * g).astype(BF16)

    y = _dot(act_ref[...], wd_ref[...])
    o_ref[0] = x + gt_ref[0] * _rms(y, gpost_ref[...])


def _ffn_call(x, sh, sc, gt, lw, tm):
    b, s, _ = x.shape
    nb = tm // SUBLANES
    row = lambda bi, i: (bi, i, 0)
    vec = lambda bi, i: (bi, 0, 0)
    weights = [lw["g_ffn_pre"], lw["g_ffn_post"], lw["w_up_a"], lw["w_up_b"],
               lw["w_conv_a"], lw["w_conv_b"], lw["b_conv_a"], lw["b_conv_b"], lw["w_down"]]
    return pl.pallas_call(
        _ffn_kernel,
        grid=(b, s // tm),
        in_specs=[pl.BlockSpec((1, SUBLANES, D_MODEL), lambda bi, i: (bi, jnp.maximum(i * nb - 1, 0), 0)),
                  pl.BlockSpec((1, tm, D_MODEL), row),
                  pl.BlockSpec((1, SUBLANES, D_MODEL),
                               lambda bi, i: (bi, jnp.minimum((i + 1) * nb, s // SUBLANES - 1), 0)),
                  pl.BlockSpec((1, 1, D_MODEL), vec),
                  pl.BlockSpec((1, 1, D_MODEL), vec),
                  pl.BlockSpec((1, 1, D_MODEL), vec)]
                 + [_const_spec(w.shape) for w in weights],
        out_specs=pl.BlockSpec((1, tm, D_MODEL), row),
        out_shape=jax.ShapeDtypeStruct(x.shape, F32),
        scratch_shapes=[pltpu.VMEM((tm, D_FF), BF16)],
        compiler_params=_params("parallel", "parallel"),
        name="ffn",
    )(x, x, x, sh, sc, gt, *weights)


def _rope_tables(s):
    half = QK_ROPE // 2
    inv = 1.0 / (ROPE_BASE ** (jnp.arange(half, dtype=F32) / half))
    ang = jnp.arange(s, dtype=F32)[:, None] * inv[None, :]
    cos, sin = jnp.cos(ang), jnp.sin(ang)
    zero = jnp.zeros((s, LANES - QK_ROPE), F32)
    return (jnp.concatenate([cos, cos, zero], axis=1),
            jnp.concatenate([-sin, sin, zero], axis=1))


def _dft_cos_sin(n):
    idx = jnp.arange(n, dtype=jnp.int32)
    ang = ((idx[:, None] * idx[None, :]) % n).astype(F32) * (2.0 * math.pi / n)
    norm = n ** -0.5
    return jnp.cos(ang) * norm, jnp.sin(ang) * norm


def _fourier_tables(s):
    cc, sc_ = _dft_cos_sin(FOURIER_GROUP)
    cs, ss = _dft_cos_sin(s)
    wcs = jnp.concatenate([cc, sc_], axis=1).astype(BF16)
    ccat = jnp.concatenate([cs, -ss], axis=1).astype(BF16)
    return wcs, ccat


def _layer_weights(l, g_mix_pre, g_mix_post, w_in, g_q, w_q_b, g_kv, w_kv_b, w_attn_o, w_four, w_out,
                   g_ffn_pre, g_ffn_post, w_up, w_conv, b_conv, w_down):
    wi = w_in[l]
    wqb = w_q_b[l].reshape(Q_LORA, N_HEADS, QK_HEAD)
    wkvb = w_kv_b[l].reshape(KV_LORA, N_HEADS, QK_NOPE + V_HEAD)
    rope_pad = ((0, 0), (0, 0), (0, LANES - QK_ROPE))
    return {
        "g_mix_pre": g_mix_pre[l].reshape(1, D_MODEL),
        "g_mix_post": g_mix_post[l].reshape(1, D_MODEL),
        "w_qa": wi[:, :OFF_KV].astype(BF16),
        "w_kva": wi[:, OFF_KV:OFF_KR].astype(BF16),
        "w_kr": jnp.pad(wi[:, OFF_KR:OFF_F], ((0, 0), (0, LANES - QK_ROPE))).astype(BF16),
        "w_f": wi[:, OFF_F:OFF_G].astype(BF16),
        "w_g": wi[:, OFF_G:].astype(BF16),
        "g_q": g_q[l].reshape(1, Q_LORA),
        "g_kv": g_kv[l].reshape(1, KV_LORA),
        "w_qn": wqb[:, :, :QK_NOPE].reshape(Q_LORA, N_HEADS * QK_NOPE).astype(BF16),
        "w_qr": jnp.pad(wqb[:, :, QK_NOPE:], rope_pad).reshape(Q_LORA, N_HEADS * LANES).astype(BF16),
        "w_kb": wkvb[:, :, :QK_NOPE].reshape(KV_LORA, N_HEADS * QK_NOPE).astype(BF16),
        "w_vbt": wkvb[:, :, QK_NOPE:].reshape(KV_LORA, N_HEADS * V_HEAD).T.astype(BF16),
        "w_attn_o": w_attn_o[l].astype(BF16),
        "w_four": w_four[l].astype(BF16),
        "w_out": w_out[l].astype(BF16),
        "g_ffn_pre": g_ffn_pre[l].reshape(1, D_MODEL),
        "g_ffn_post": g_ffn_post[l].reshape(1, D_MODEL),
        "w_up_a": w_up[l][:, :D_FF].astype(BF16),
        "w_up_b": w_up[l][:, D_FF:].astype(BF16),
        "w_conv_a": w_conv[l][:, :D_FF],
        "w_conv_b": w_conv[l][:, D_FF:],
        "b_conv_a": b_conv[l][:D_FF].reshape(1, D_FF),
        "b_conv_b": b_conv[l][D_FF:].reshape(1, D_FF),
        "w_down": w_down[l].astype(BF16),
    }


def _tile(s, want):
    t = min(s, want)
    assert s % t == 0
    return t


def _trunk(x, c, w_ada, b_ada, layers):
    b, s, _ = x.shape
    tm = _tile(s, 512)
    tq = _tile(s, 512)
    tr = _tile(s, 256)
    cos_t, sin_t = _rope_tables(s)
    wcs, ccat = _fourier_tables(s)
    mods = _ada_mods(c, w_ada, b_ada).reshape(DEPTH, b, N_ADA, 1, D_MODEL)
    for l, lw in enumerate(layers):
        sh1, sc1, gt1, sh2, sc2, gt2 = [mods[l, :, j] for j in range(N_ADA)]
        q, k, vt, u, gates = _pre_call(x, sh1, sc1, lw, cos_t, sin_t, tm)
        four = _four_call(u, wcs, ccat, tr)
        attn = _attn_call(q, k, vt, tq)
        x = _merge_call(attn, four, gates, x, gt1, lw, tm)
        x = _ffn_call(x, sh2, sc2, gt2, lw, tm)
    return x


def kernel(x_prompt, x_sample, c_prompt, c_sample, w_ada, b_ada, g_mix_pre, g_mix_post, w_in, g_q, w_q_b,
           g_kv, w_kv_b, w_attn_o, w_four, w_out, g_ffn_pre, g_ffn_post, w_up, w_conv, b_conv, w_down):
    layers = [_layer_weights(l, g_mix_pre, g_mix_post, w_in, g_q, w_q_b, g_kv, w_kv_b, w_attn_o, w_four,
                             w_out, g_ffn_pre, g_ffn_post, w_up, w_conv, b_conv, w_down)
              for l in range(DEPTH)]
    y_prompt = _trunk(x_prompt, c_prompt, w_ada, b_ada, layers)
    y_sample = _trunk(x_sample, c_sample, w_ada, b_ada, layers)
    return (y_prompt, y_sample)
```

```python
import functools
import math

import jax
import jax.numpy as jnp
from jax import lax
from jax.experimental import pallas as pl
from jax.experimental.pallas import tpu as pltpu

D_MODEL = 1024
DEPTH = 4
N_HEADS = 8
QK_NOPE = 128
QK_ROPE = 64
V_HEAD = 128
Q_LORA = 512
KV_LORA = 256
QK_HEAD = QK_NOPE + QK_ROPE
ROPE_BASE = 10000.0
N_FOURIER_GROUPS = 4
FOURIER_GROUP = 128
FOURIER_WIDTH = N_FOURIER_GROUPS * FOURIER_GROUP
D_FF = 2816
EPS = 1e-6
N_ADA = 6
OFF_KV = Q_LORA
OFF_KR = OFF_KV + KV_LORA
OFF_F = OFF_KR + QK_ROPE
OFF_G = OFF_F + FOURIER_WIDTH

LANES = 128
SUBLANES = 8
QK_PAD = 2 * LANES
BF16_SUBLANES = 16
V_AUG = V_HEAD + BF16_SUBLANES
VMEM_LIMIT_BYTES = 56 * 1024 * 1024

BF16 = jnp.bfloat16
F32 = jnp.float32


def _dot(a, b):
    return jnp.dot(a, b, preferred_element_type=F32)


def _dot_nt(a, b):
    return lax.dot_general(a, b, (((1,), (1,)), ((), ())), preferred_element_type=F32)


def _rms(x, g):
    return x * lax.rsqrt(jnp.mean(x * x, axis=-1, keepdims=True) + EPS) * g


def _const_spec(shape):
    nd = len(shape)
    return pl.BlockSpec(shape, lambda *_: (0,) * nd, pipeline_mode=pl.Buffered(1))


def _params(*sem):
    return pltpu.CompilerParams(dimension_semantics=sem, vmem_limit_bytes=VMEM_LIMIT_BYTES)


def _ada_kernel(c_ref, w_ref, b_ref, o_ref):
    c = c_ref[...]
    s = c * jax.nn.sigmoid(c)
    o_ref[0] = jnp.dot(s, w_ref[0], preferred_element_type=F32,
                       precision=lax.Precision.HIGHEST) + b_ref[0]


def _ada_mods(c, w_ada, b_ada):
    b = c.shape[0]
    bp = -(-b // SUBLANES) * SUBLANES
    cp = jnp.pad(c, ((0, bp - b), (0, 0)))
    out = pl.pallas_call(
        _ada_kernel,
        grid=(DEPTH, N_ADA),
        in_specs=[
            pl.BlockSpec((bp, D_MODEL), lambda l, j: (0, 0)),
            pl.BlockSpec((1, D_MODEL, D_MODEL), lambda l, j: (l, 0, j)),
            pl.BlockSpec((1, 1, D_MODEL), lambda l, j: (l, 0, j)),
        ],
        out_specs=pl.BlockSpec((1, bp, D_MODEL), lambda l, j: (l, 0, j)),
        out_shape=jax.ShapeDtypeStruct((DEPTH, bp, N_ADA * D_MODEL), F32),
        compiler_params=_params("arbitrary", "arbitrary"),
        name="ada",
    )(cp, w_ada, b_ada.reshape(DEPTH, 1, N_ADA * D_MODEL))
    return out[:, :b]


def _pre_kernel(x_ref, sh_ref, sc_ref, gpre_ref, wqa_ref, wkva_ref, wkr_ref, wf_ref, wg_ref,
                gq_ref, gkv_ref, wqn_ref, wqr_ref, wkb_ref, wvbt_ref, cos_ref, sin_ref,
                q_ref, k_ref, vt_ref, u_ref, g_ref):
    tm = x_ref.shape[1]
    x = x_ref[0]
    h = _rms(x, gpre_ref[...]) * (1.0 + sc_ref[0]) + sh_ref[0]
    hb = h.astype(BF16)

    u_ref[0] = _dot(hb, wf_ref[...]).astype(BF16)
    g_ref[0] = _dot(hb, wg_ref[...]).astype(BF16)
    cq = _rms(_dot(hb, wqa_ref[...]), gq_ref[...]).astype(BF16)
    ckv = _rms(_dot(hb, wkva_ref[...]), gkv_ref[...]).astype(BF16)

    cos = cos_ref[...]
    sin = sin_ref[...]
    lane = lax.broadcasted_iota(jnp.int32, (tm, LANES), 1)
    low_half = (lane % QK_ROPE) < (QK_ROPE // 2)

    def rope(t):
        rot = jnp.where(low_half, pltpu.roll(t, LANES - QK_ROPE // 2, 1), pltpu.roll(t, QK_ROPE // 2, 1))
        return t * cos + rot * sin

    scale = QK_HEAD ** -0.5 * math.log2(math.e)
    kr = rope(_dot(hb, wkr_ref[...])).astype(BF16)
    qn = _dot(cq, wqn_ref[...])
    qr = _dot(cq, wqr_ref[...])
    kn = _dot(ckv, wkb_ref[...])
    for hd in range(N_HEADS):
        lo = hd * QK_PAD
        sl = slice(hd * LANES, (hd + 1) * LANES)
        q_ref[0, :, lo:lo + LANES] = (qn[:, sl] * scale).astype(BF16)
        q_ref[0, :, lo + LANES:lo + QK_PAD] = (rope(qr[:, sl]) * scale).astype(BF16)
        k_ref[0, :, lo:lo + LANES] = kn[:, sl].astype(BF16)
        k_ref[0, :, lo + LANES:lo + QK_PAD] = kr
    vt = _dot_nt(wvbt_ref[...], ckv)
    extra = lax.broadcasted_iota(jnp.int32, (V_AUG - V_HEAD, tm), 0)
    ones_row = jnp.where(extra == 0, 1.0, 0.0).astype(BF16)
    for hd in range(N_HEADS):
        vt_ref[0, hd * V_AUG:hd * V_AUG + V_HEAD, :] = vt[hd * V_HEAD:(hd + 1) * V_HEAD].astype(BF16)
        vt_ref[0, hd * V_AUG + V_HEAD:(hd + 1) * V_AUG, :] = ones_row


def _pre_call(x, sh, sc, lw, cos_t, sin_t, tm):
    b, s, _ = x.shape
    row = lambda bi, i: (bi, i, 0)
    vec = lambda bi, i: (bi, 0, 0)
    weights = [lw["g_mix_pre"], lw["w_qa"], lw["w_kva"], lw["w_kr"], lw["w_f"], lw["w_g"],
               lw["g_q"], lw["g_kv"], lw["w_qn"], lw["w_qr"], lw["w_kb"], lw["w_vbt"]]
    return pl.pallas_call(
        _pre_kernel,
        grid=(b, s // tm),
        in_specs=[pl.BlockSpec((1, tm, D_MODEL), row),
                  pl.BlockSpec((1, 1, D_MODEL), vec),
                  pl.BlockSpec((1, 1, D_MODEL), vec)]
                 + [_const_spec(w.shape) for w in weights]
                 + [pl.BlockSpec((tm, LANES), lambda bi, i: (i, 0)),
                    pl.BlockSpec((tm, LANES), lambda bi, i: (i, 0))],
        out_specs=[pl.BlockSpec((1, tm, N_HEADS * QK_PAD), row),
                   pl.BlockSpec((1, tm, N_HEADS * QK_PAD), row),
                   pl.BlockSpec((1, N_HEADS * V_AUG, tm), lambda bi, i: (bi, 0, i)),
                   pl.BlockSpec((1, tm, FOURIER_WIDTH), row),
                   pl.BlockSpec((1, tm, 2 * D_MODEL), row)],
        out_shape=[jax.ShapeDtypeStruct((b, s, N_HEADS * QK_PAD), BF16),
                   jax.ShapeDtypeStruct((b, s, N_HEADS * QK_PAD), BF16),
                   jax.ShapeDtypeStruct((b, N_HEADS * V_AUG, s), BF16),
                   jax.ShapeDtypeStruct((b, s, FOURIER_WIDTH), BF16),
                   jax.ShapeDtypeStruct((b, s, 2 * D_MODEL), BF16)],
        compiler_params=_params("parallel", "parallel"),
        name="pre",
    )(x, sh, sc, *weights, cos_t, sin_t)


def _four_kernel(u_ref, wcs_ref, ccat_ref, o_ref, ab_ref):
    s = u_ref.shape[1]

    @pl.when(pl.program_id(1) == 0)
    def _():
        for g in range(N_FOURIER_GROUPS):
            sl = slice(g * FOURIER_GROUP, (g + 1) * FOURIER_GROUP)
            ab = _dot(u_ref[0, :, sl], wcs_ref[...])
            ab_ref[0:s, sl] = ab[:, :FOURIER_GROUP].astype(BF16)
            ab_ref[s:2 * s, sl] = ab[:, FOURIER_GROUP:].astype(BF16)

    o_ref[0] = _dot(ccat_ref[...], ab_ref[...]).astype(BF16)


def _four_call(u, wcs, ccat, tr):
    b, s, _ = u.shape
    return pl.pallas_call(
        _four_kernel,
        grid=(b, s // tr),
        in_specs=[pl.BlockSpec((1, s, FOURIER_WIDTH), lambda bi, r: (bi, 0, 0)),
                  _const_spec(wcs.shape),
                  pl.BlockSpec((tr, 2 * s), lambda bi, r: (r, 0))],
        out_specs=pl.BlockSpec((1, tr, FOURIER_WIDTH), lambda bi, r: (bi, r, 0)),
        out_shape=jax.ShapeDtypeStruct((b, s, FOURIER_WIDTH), BF16),
        scratch_shapes=[pltpu.VMEM((2 * s, FOURIER_WIDTH), BF16)],
        compiler_params=_params("parallel", "arbitrary"),
        name="four",
    )(u, wcs, ccat)


def _attn_kernel(q_ref, k_ref, vt_ref, o_ref, s0_ref, s1_ref):
    tq = s0_ref.shape[1]
    nq = q_ref.shape[1] // tq
    bufs = (s0_ref, s1_ref)

    def scores(j):
        st = _dot_nt(k_ref[0], q_ref[0, j * tq:(j + 1) * tq, :])
        bufs[j % 2][...] = st
        return jnp.max(st, axis=0, keepdims=True)

    def finish(j, m):
        p = jnp.exp2(bufs[j % 2][...] - m)
        ot = _dot(vt_ref[0], p.astype(BF16))
        inv_l = 1.0 / ot[V_HEAD:V_HEAD + 1]
        o_ref[0, j * tq:(j + 1) * tq, :] = (ot[:V_HEAD] * inv_l).T.astype(BF16)

    m = scores(0)
    for j in range(nq):
        m_next = scores(j + 1) if j + 1 < nq else None
        finish(j, m)
        m = m_next


def _attn_call(q, k, vt, tq):
    b, s, _ = q.shape
    return pl.pallas_call(
        _attn_kernel,
        grid=(b, N_HEADS),
        in_specs=[pl.BlockSpec((1, s, QK_PAD), lambda bi, h: (bi, 0, h)),
                  pl.BlockSpec((1, s, QK_PAD), lambda bi, h: (bi, 0, h)),
                  pl.BlockSpec((1, V_AUG, s), lambda bi, h: (bi, h, 0))],
        out_specs=pl.BlockSpec((1, s, V_HEAD), lambda bi, h: (bi, 0, h)),
        out_shape=jax.ShapeDtypeStruct((b, s, N_HEADS * V_HEAD), BF16),
        scratch_shapes=[pltpu.VMEM((s, tq), F32), pltpu.VMEM((s, tq), F32)],
        compiler_params=_params("parallel", "parallel"),
        name="attn",
    )(q, k, vt)


def _merge_kernel(a_ref, f_ref, g_ref, x_ref, gt_ref, gpost_ref, wao_ref, wfo_ref, wout_ref, o_ref):
    branch_a = _dot(a_ref[0], wao_ref[...])
    branch_b = _dot(f_ref[0], wfo_ref[...])
    gate_a = g_ref[0, :, :D_MODEL].astype(F32)
    gate_b = g_ref[0, :, D_MODEL:].astype(F32)
    merged = jax.nn.sigmoid(gate_a) * branch_a + jax.nn.sigmoid(gate_b) * branch_b
    y = _dot(merged.astype(BF16), wout_ref[...])
    o_ref[0] = x_ref[0] + gt_ref[0] * _rms(y, gpost_ref[...])


def _merge_call(attn, four, gates, x, gt, lw, tm):
    b, s, _ = x.shape
    row = lambda bi, i: (bi, i, 0)
    weights = [lw["g_mix_post"], lw["w_attn_o"], lw["w_four"], lw["w_out"]]
    return pl.pallas_call(
        _merge_kernel,
        grid=(b, s // tm),
        in_specs=[pl.BlockSpec((1, tm, N_HEADS * V_HEAD), row),
                  pl.BlockSpec((1, tm, FOURIER_WIDTH), row),
                  pl.BlockSpec((1, tm, 2 * D_MODEL), row),
                  pl.BlockSpec((1, tm, D_MODEL), row),
                  pl.BlockSpec((1, 1, D_MODEL), lambda bi, i: (bi, 0, 0))]
                 + [_const_spec(w.shape) for w in weights],
        out_specs=pl.BlockSpec((1, tm, D_MODEL), row),
        out_shape=jax.ShapeDtypeStruct(x.shape, F32),
        compiler_params=_params("parallel", "parallel"),
        name="merge",
    )(attn, four, gates, x, gt, *weights)


FFN_CHUNK = 2 * LANES
assert D_FF % FFN_CHUNK == 0
GELU_C = math.sqrt(2.0 / math.pi)


def _gelu_tanh(a):
    k1 = -2.0 * GELU_C * math.log2(math.e)
    k3 = k1 * 0.044715
    return a / (1.0 + jnp.exp2(a * (k1 + k3 * (a * a))))


def _ffn_kernel(xp_ref, x_ref, xn_ref, sh_ref, sc_ref, gt_ref, gpre_ref, gpost_ref,
                wua_ref, wub_ref, wca_ref, wcb_ref, bca_ref, bcb_ref, wd_ref, o_ref, act_ref):
    i = pl.program_id(1)
    last = pl.num_programs(1) - 1
    tm = x_ref.shape[1]
    te = tm + 2 * SUBLANES
    x = x_ref[0]
    xe = jnp.concatenate([xp_ref[0], x, xn_ref[0]], axis=0)
    he = _rms(xe, gpre_ref[...]) * (1.0 + sc_ref[0]) + sh_ref[0]
    r = lax.broadcasted_iota(jnp.int32, (te, 1), 0)
    inside = jnp.logical_and(jnp.logical_or(r >= SUBLANES, i > 0),
                             jnp.logical_or(r < tm + SUBLANES, i < last))
    he = jnp.where(inside, he, 0.0).astype(BF16)

    def conv(u, w_ref, b_ref, cs):
        prev = pltpu.roll(u, 1, 0)[SUBLANES:SUBLANES + tm]
        nxt = pltpu.roll(u, te - 1, 0)[SUBLANES:SUBLANES + tm]
        mid = u[SUBLANES:SUBLANES + tm]
        return prev * w_ref[0:1, cs] + mid * w_ref[1:2, cs] + nxt * w_ref[2:3, cs] + b_ref[:, cs]

    for c in range(D_FF // FFN_CHUNK):
        cs = slice(c * FFN_CHUNK, (c + 1) * FFN_CHUNK)
        a = conv(_dot(he, wua_ref[:, cs]), wca_ref, bca_ref, cs)
        g = conv(_dot(he, wub_ref[:, cs]), wcb_ref, bcb_ref, cs)
        act_ref[:, cs] = (_gelu_tanh(a) * g).astype(BF16)

    y = _dot(act_ref[...], wd_ref[...])
    o_ref[0] = x + gt_ref[0] * _rms(y, gpost_ref[...])


def _ffn_call(x, sh, sc, gt, lw, tm):
    b, s, _ = x.shape
    nb = tm // SUBLANES
    row = lambda bi, i: (bi, i, 0)
    vec = lambda bi, i: (bi, 0, 0)
    weights = [lw["g_ffn_pre"], lw["g_ffn_post"], lw["w_up_a"], lw["w_up_b"],
               lw["w_conv_a"], lw["w_conv_b"], lw["b_conv_a"], lw["b_conv_b"], lw["w_down"]]
    return pl.pallas_call(
        _ffn_kernel,
        grid=(b, s // tm),
        in_specs=[pl.BlockSpec((1, SUBLANES, D_MODEL), lambda bi, i: (bi, jnp.maximum(i * nb - 1, 0), 0)),
                  pl.BlockSpec((1, tm, D_MODEL), row),
                  pl.BlockSpec((1, SUBLANES, D_MODEL),
                               lambda bi, i: (bi, jnp.minimum((i + 1) * nb, s // SUBLANES - 1), 0)),
                  pl.BlockSpec((1, 1, D_MODEL), vec),
                  pl.BlockSpec((1, 1, D_MODEL), vec),
                  pl.BlockSpec((1, 1, D_MODEL), vec)]
                 + [_const_spec(w.shape) for w in weights],
        out_specs=pl.BlockSpec((1, tm, D_MODEL), row),
        out_shape=jax.ShapeDtypeStruct(x.shape, F32),
        scratch_shapes=[pltpu.VMEM((tm, D_FF), BF16)],
        compiler_params=_params("parallel", "parallel"),
        name="ffn",
    )(x, x, x, sh, sc, gt, *weights)


def _rope_tables(s):
    half = QK_ROPE // 2
    inv = 1.0 / (ROPE_BASE ** (jnp.arange(half, dtype=F32) / half))
    ang = jnp.arange(s, dtype=F32)[:, None] * inv[None, :]
    cos, sin = jnp.cos(ang), jnp.sin(ang)
    zero = jnp.zeros((s, LANES - QK_ROPE), F32)
    return (jnp.concatenate([cos, cos, zero], axis=1),
            jnp.concatenate([-sin, sin, zero], axis=1))


def _dft_cos_sin(n):
    idx = jnp.arange(n, dtype=jnp.int32)
    ang = ((idx[:, None] * idx[None, :]) % n).astype(F32) * (2.0 * math.pi / n)
    norm = n ** -0.5
    return jnp.cos(ang) * norm, jnp.sin(ang) * norm


def _fourier_tables(s):
    cc, sc_ = _dft_cos_sin(FOURIER_GROUP)
    cs, ss = _dft_cos_sin(s)
    wcs = jnp.concatenate([cc, sc_], axis=1).astype(BF16)
    ccat = jnp.concatenate([cs, -ss], axis=1).astype(BF16)
    return wcs, ccat


def _layer_weights(l, g_mix_pre, g_mix_post, w_in, g_q, w_q_b, g_kv, w_kv_b, w_attn_o, w_four, w_out,
                   g_ffn_pre, g_ffn_post, w_up, w_conv, b_conv, w_down):
    wi = w_in[l]
    wqb = w_q_b[l].reshape(Q_LORA, N_HEADS, QK_HEAD)
    wkvb = w_kv_b[l].reshape(KV_LORA, N_HEADS, QK_NOPE + V_HEAD)
    rope_pad = ((0, 0), (0, 0), (0, LANES - QK_ROPE))
    return {
        "g_mix_pre": g_mix_pre[l].reshape(1, D_MODEL),
        "g_mix_post": g_mix_post[l].reshape(1, D_MODEL),
        "w_qa": wi[:, :OFF_KV].astype(BF16),
        "w_kva": wi[:, OFF_KV:OFF_KR].astype(BF16),
        "w_kr": jnp.pad(wi[:, OFF_KR:OFF_F], ((0, 0), (0, LANES - QK_ROPE))).astype(BF16),
        "w_f": wi[:, OFF_F:OFF_G].astype(BF16),
        "w_g": wi[:, OFF_G:].astype(BF16),
        "g_q": g_q[l].reshape(1, Q_LORA),
        "g_kv": g_kv[l].reshape(1, KV_LORA),
        "w_qn": wqb[:, :, :QK_NOPE].reshape(Q_LORA, N_HEADS * QK_NOPE).astype(BF16),
        "w_qr": jnp.pad(wqb[:, :, QK_NOPE:], rope_pad).reshape(Q_LORA, N_HEADS * LANES).astype(BF16),
        "w_kb": wkvb[:, :, :QK_NOPE].reshape(KV_LORA, N_HEADS * QK_NOPE).astype(BF16),
        "w_vbt": wkvb[:, :, QK_NOPE:].reshape(KV_LORA, N_HEADS * V_HEAD).T.astype(BF16),
        "w_attn_o": w_attn_o[l].astype(BF16),
        "w_four": w_four[l].astype(BF16),
        "w_out": w_out[l].astype(BF16),
        "g_ffn_pre": g_ffn_pre[l].reshape(1, D_MODEL),
        "g_ffn_post": g_ffn_post[l].reshape(1, D_MODEL),
        "w_up_a": w_up[l][:, :D_FF].astype(BF16),
        "w_up_b": w_up[l][:, D_FF:].astype(BF16),
        "w_conv_a": w_conv[l][:, :D_FF],
        "w_conv_b": w_conv[l][:, D_FF:],
        "b_conv_a": b_conv[l][:D_FF].reshape(1, D_FF),
        "b_conv_b": b_conv[l][D_FF:].reshape(1, D_FF),
        "w_down": w_down[l].astype(BF16),
    }


def _tile(s, want):
    t = min(s, want)
    assert s % t == 0
    return t


def _trunk(x, c, w_ada, b_ada, layers):
    b, s, _ = x.shape
    tm = _tile(s, 512)
    tq = _tile(s, 512)
    tr = _tile(s, 512)
    cos_t, sin_t = _rope_tables(s)
    wcs, ccat = _fourier_tables(s)
    mods = _ada_mods(c, w_ada, b_ada).reshape(DEPTH, b, N_ADA, 1, D_MODEL)
    for l, lw in enumerate(layers):
        sh1, sc1, gt1, sh2, sc2, gt2 = [mods[l, :, j] for j in range(N_ADA)]
        q, k, vt, u, gates = _pre_call(x, sh1, sc1, lw, cos_t, sin_t, tm)
        four = _four_call(u, wcs, ccat, tr)
        attn = _attn_call(q, k, vt, tq)
        x = _merge_call(attn, four, gates, x, gt1, lw, tm)
        x = _ffn_call(x, sh2, sc2, gt2, lw, tm)
    return x


def kernel(x_prompt, x_sample, c_prompt, c_sample, w_ada, b_ada, g_mix_pre, g_mix_post, w_in, g_q, w_q_b,
           g_kv, w_kv_b, w_attn_o, w_four, w_out, g_ffn_pre, g_ffn_post, w_up, w_conv, b_conv, w_down):
    layers = [_layer_weights(l, g_mix_pre, g_mix_post, w_in, g_q, w_q_b, g_kv, w_kv_b, w_attn_o, w_four,
                             w_out, g_ffn_pre, g_ffn_post, w_up, w_conv, b_conv, w_down)
              for l in range(DEPTH)]
    y_prompt = _trunk(x_prompt, c_prompt, w_ada, b_ada, layers)
    y_sample = _trunk(x_sample, c_sample, w_ada, b_ada, layers)
    return (y_prompt, y_sample)
```

```python
import functools
import math

import jax
import jax.numpy as jnp
from jax import lax
from jax.experimental import pallas as pl
from jax.experimental.pallas import tpu as pltpu

D_MODEL = 1024
DEPTH = 4
N_HEADS = 8
QK_NOPE = 128
QK_ROPE = 64
V_HEAD = 128
Q_LORA = 512
KV_LORA = 256
QK_HEAD = QK_NOPE + QK_ROPE
ROPE_BASE = 10000.0
N_FOURIER_GROUPS = 4
FOURIER_GROUP = 128
FOURIER_WIDTH = N_FOURIER_GROUPS * FOURIER_GROUP
D_FF = 2816
EPS = 1e-6
N_ADA = 6
OFF_KV = Q_LORA
OFF_KR = OFF_KV + KV_LORA
OFF_F = OFF_KR + QK_ROPE
OFF_G = OFF_F + FOURIER_WIDTH

LANES = 128
SUBLANES = 8
QK_PAD = 2 * LANES
BF16_SUBLANES = 16
V_AUG = V_HEAD + BF16_SUBLANES
VMEM_LIMIT_BYTES = 56 * 1024 * 1024

BF16 = jnp.bfloat16
F32 = jnp.float32


def _dot(a, b):
    return jnp.dot(a, b, preferred_element_type=F32)


def _dot_nt(a, b):
    return lax.dot_general(a, b, (((1,), (1,)), ((), ())), preferred_element_type=F32)


def _rms(x, g):
    return x * lax.rsqrt(jnp.mean(x * x, axis=-1, keepdims=True) + EPS) * g


def _const_spec(shape):
    nd = len(shape)
    return pl.BlockSpec(shape, lambda *_: (0,) * nd, pipeline_mode=pl.Buffered(1))


def _params(*sem):
    return pltpu.CompilerParams(dimension_semantics=sem, vmem_limit_bytes=VMEM_LIMIT_BYTES)


def _ada_kernel(c_ref, w_ref, b_ref, o_ref):
    c = c_ref[...]
    s = c * jax.nn.sigmoid(c)
    o_ref[0] = jnp.dot(s, w_ref[0], preferred_element_type=F32,
                       precision=lax.Precision.HIGHEST) + b_ref[0]


def _ada_mods(c, w_ada, b_ada):
    b = c.shape[0]
    bp = -(-b // SUBLANES) * SUBLANES
    cp = jnp.pad(c, ((0, bp - b), (0, 0)))
    out = pl.pallas_call(
        _ada_kernel,
        grid=(DEPTH, N_ADA),
        in_specs=[
            pl.BlockSpec((bp, D_MODEL), lambda l, j: (0, 0)),
            pl.BlockSpec((1, D_MODEL, D_MODEL), lambda l, j: (l, 0, j)),
            pl.BlockSpec((1, 1, D_MODEL), lambda l, j: (l, 0, j)),
        ],
        out_specs=pl.BlockSpec((1, bp, D_MODEL), lambda l, j: (l, 0, j)),
        out_shape=jax.ShapeDtypeStruct((DEPTH, bp, N_ADA * D_MODEL), F32),
        compiler_params=_params("arbitrary", "arbitrary"),
        name="ada",
    )(cp, w_ada, b_ada.reshape(DEPTH, 1, N_ADA * D_MODEL))
    return out[:, :b]


def _pre_kernel(x_ref, sh_ref, sc_ref, gpre_ref, wqa_ref, wkva_ref, wkr_ref, wf_ref, wg_ref,
                gq_ref, gkv_ref, wqn_ref, wqr_ref, wkb_ref, wvbt_ref, hsum_ref, cos_ref, sin_ref,
                q_ref, k_ref, vt_ref, u_ref, g_ref, qsq_ref, ksq_ref):
    tm = x_ref.shape[1]
    x = x_ref[0]
    h = _rms(x, gpre_ref[...]) * (1.0 + sc_ref[0]) + sh_ref[0]
    hb = h.astype(BF16)

    u_ref[0] = _dot(hb, wf_ref[...]).astype(BF16)
    g_ref[0] = _dot(hb, wg_ref[...]).astype(BF16)
    cq = _rms(_dot(hb, wqa_ref[...]), gq_ref[...]).astype(BF16)
    ckv = _rms(_dot(hb, wkva_ref[...]), gkv_ref[...]).astype(BF16)

    cos = cos_ref[...]
    sin = sin_ref[...]
    lane = lax.broadcasted_iota(jnp.int32, (tm, LANES), 1)
    low_half = (lane % QK_ROPE) < (QK_ROPE // 2)

    def rope(t):
        rot = jnp.where(low_half, pltpu.roll(t, LANES - QK_ROPE // 2, 1), pltpu.roll(t, QK_ROPE // 2, 1))
        return t * cos + rot * sin

    scale = QK_HEAD ** -0.5 * math.log2(math.e)
    kr = rope(_dot(hb, wkr_ref[...])).astype(BF16)
    qn = _dot(cq, wqn_ref[...])
    qr = _dot(cq, wqr_ref[...])
    kn = _dot(ckv, wkb_ref[...])

    def sq(t):
        t = t.astype(F32)
        return t * t

    kr_sq = sq(kr)
    q_sq, k_sq = [], []
    for hd in range(N_HEADS):
        lo = hd * QK_PAD
        sl = slice(hd * LANES, (hd + 1) * LANES)
        q_nope = (qn[:, sl] * scale).astype(BF16)
        q_rope = (rope(qr[:, sl]) * scale).astype(BF16)
        k_nope = kn[:, sl].astype(BF16)
        q_ref[0, :, lo:lo + LANES] = q_nope
        q_ref[0, :, lo + LANES:lo + QK_PAD] = q_rope
        k_ref[0, :, lo:lo + LANES] = k_nope
        k_ref[0, :, lo + LANES:lo + QK_PAD] = kr
        q_sq.append((sq(q_nope) + sq(q_rope)).astype(BF16))
        k_sq.append((sq(k_nope) + kr_sq).astype(BF16))
    qsq_ref[0] = _dot_nt(hsum_ref[...], jnp.concatenate(q_sq, axis=1))[:N_HEADS]
    ksq_ref[0] = _dot_nt(hsum_ref[...], jnp.concatenate(k_sq, axis=1))[:N_HEADS]
    vt = _dot_nt(wvbt_ref[...], ckv)
    extra = lax.broadcasted_iota(jnp.int32, (V_AUG - V_HEAD, tm), 0)
    ones_row = jnp.where(extra == 0, 1.0, 0.0).astype(BF16)
    for hd in range(N_HEADS):
        vt_ref[0, hd * V_AUG:hd * V_AUG + V_HEAD, :] = vt[hd * V_HEAD:(hd + 1) * V_HEAD].astype(BF16)
        vt_ref[0, hd * V_AUG + V_HEAD:(hd + 1) * V_AUG, :] = ones_row


def _head_sum_matrix():
    r = jnp.arange(BF16_SUBLANES, dtype=jnp.int32)[:, None]
    c = jnp.arange(N_HEADS * LANES, dtype=jnp.int32)[None, :] // LANES
    return (r == c).astype(BF16)


def _pre_call(x, sh, sc, lw, cos_t, sin_t, tm):
    b, s, _ = x.shape
    row = lambda bi, i: (bi, i, 0)
    vec = lambda bi, i: (bi, 0, 0)
    weights = [lw["g_mix_pre"], lw["w_qa"], lw["w_kva"], lw["w_kr"], lw["w_f"], lw["w_g"],
               lw["g_q"], lw["g_kv"], lw["w_qn"], lw["w_qr"], lw["w_kb"], lw["w_vbt"], _head_sum_matrix()]
    return pl.pallas_call(
        _pre_kernel,
        grid=(b, s // tm),
        in_specs=[pl.BlockSpec((1, tm, D_MODEL), row),
                  pl.BlockSpec((1, 1, D_MODEL), vec),
                  pl.BlockSpec((1, 1, D_MODEL), vec)]
                 + [_const_spec(w.shape) for w in weights]
                 + [pl.BlockSpec((tm, LANES), lambda bi, i: (i, 0)),
                    pl.BlockSpec((tm, LANES), lambda bi, i: (i, 0))],
        out_specs=[pl.BlockSpec((1, tm, N_HEADS * QK_PAD), row),
                   pl.BlockSpec((1, tm, N_HEADS * QK_PAD), row),
                   pl.BlockSpec((1, N_HEADS * V_AUG, tm), lambda bi, i: (bi, 0, i)),
                   pl.BlockSpec((1, tm, FOURIER_WIDTH), row),
                   pl.BlockSpec((1, tm, 2 * D_MODEL), row),
                   pl.BlockSpec((1, N_HEADS, tm), lambda bi, i: (bi, 0, i)),
                   pl.BlockSpec((1, N_HEADS, tm), lambda bi, i: (bi, 0, i))],
        out_shape=[jax.ShapeDtypeStruct((b, s, N_HEADS * QK_PAD), BF16),
                   jax.ShapeDtypeStruct((b, s, N_HEADS * QK_PAD), BF16),
                   jax.ShapeDtypeStruct((b, N_HEADS * V_AUG, s), BF16),
                   jax.ShapeDtypeStruct((b, s, FOURIER_WIDTH), BF16),
                   jax.ShapeDtypeStruct((b, s, 2 * D_MODEL), BF16),
                   jax.ShapeDtypeStruct((b, N_HEADS, s), F32),
                   jax.ShapeDtypeStruct((b, N_HEADS, s), F32)],
        compiler_params=_params("parallel", "parallel"),
        name="pre",
    )(x, sh, sc, *weights, cos_t, sin_t)


DFT_N2 = 16
DFT_JG = BF16_SUBLANES
DFT_PB = DFT_N2 * BF16_SUBLANES


def _four_kernel(u_ref, perm_ref, wcs_ref, w1_ref, bd_ref, o_ref, ab_ref, z_ref):
    s = u_ref.shape[1]
    n1 = s // DFT_N2
    rows = BF16_SUBLANES
    for blk in range(s // DFT_PB):
        ub = u_ref[0, blk * DFT_PB:(blk + 1) * DFT_PB, :]
        up = _dot(perm_ref[...], ub).astype(BF16)
        for g in range(N_FOURIER_GROUPS):
            sl = slice(g * FOURIER_GROUP, (g + 1) * FOURIER_GROUP)
            ab = _dot(up[:, sl], wcs_ref[...]).astype(BF16)
            for s2 in range(DFT_N2):
                src = slice(s2 * rows, (s2 + 1) * rows)
                dst = slice(blk * rows, (blk + 1) * rows)
                ab_ref[s2, dst, sl] = ab[src, :FOURIER_GROUP]
                ab_ref[s2, n1 + blk * rows:n1 + (blk + 1) * rows, sl] = ab[src, FOURIER_GROUP:]
    kg = 2 * DFT_N2 * DFT_JG
    for s2 in range(DFT_N2):
        z = _dot(w1_ref[s2], ab_ref[s2]).astype(BF16)
        for grp in range(n1 // DFT_JG):
            for r in range(2):
                dst = grp * kg + r * DFT_N2 * DFT_JG + s2 * DFT_JG
                z_ref[dst:dst + DFT_JG, :] = z[r * n1 + grp * DFT_JG:r * n1 + (grp + 1) * DFT_JG, :]
    for grp in range(n1 // DFT_JG):
        y = _dot(bd_ref[...], z_ref[grp * kg:(grp + 1) * kg, :]).astype(BF16)
        for j2 in range(DFT_N2):
            dst = n1 * j2 + grp * DFT_JG
            o_ref[0, dst:dst + DFT_JG, :] = y[j2 * DFT_JG:(j2 + 1) * DFT_JG, :]


def _four_call(u, tables):
    b, s, _ = u.shape
    n1 = s // DFT_N2
    return pl.pallas_call(
        _four_kernel,
        grid=(b,),
        in_specs=[pl.BlockSpec((1, s, FOURIER_WIDTH), lambda bi: (bi, 0, 0))]
                 + [_const_spec(t.shape) for t in tables],
        out_specs=pl.BlockSpec((1, s, FOURIER_WIDTH), lambda bi: (bi, 0, 0)),
        out_shape=jax.ShapeDtypeStruct((b, s, FOURIER_WIDTH), BF16),
        scratch_shapes=[pltpu.VMEM((DFT_N2, 2 * n1, FOURIER_WIDTH), BF16),
                        pltpu.VMEM((2 * s, FOURIER_WIDTH), BF16)],
        compiler_params=_params("parallel"),
        name="four",
    )(u, *tables)


ATTN_MIN_SUM = 2.0 ** -64
ATTN_BOUND_SLACK = 1.0 + 2.0 ** -6


def _attn_kernel(q_ref, k_ref, vt_ref, qsq_ref, ksq_ref, o_ref, *, tq):
    nq = q_ref.shape[1] // tq

    def tile(cols, m_of):
        st = _dot_nt(k_ref[0], q_ref[0, cols, :])
        p = jnp.exp2(st - m_of(st))
        ot = _dot(vt_ref[0], p.astype(BF16))
        l = ot[V_HEAD:V_HEAD + 1]
        o_ref[0, cols, :] = (ot[:V_HEAD] * (1.0 / l)).T.astype(BF16)
        return l

    k_max = jnp.sqrt(jnp.max(ksq_ref[0, 0], axis=-1, keepdims=True)) * ATTN_BOUND_SLACK
    l_min = None
    for j in range(nq):
        cols = slice(j * tq, (j + 1) * tq)
        bound = jnp.sqrt(qsq_ref[0, 0, :, cols]) * k_max
        l = tile(cols, lambda st: bound)
        l_min = l if l_min is None else jnp.minimum(l_min, l)

    @pl.when(jnp.logical_not(jnp.min(l_min) >= ATTN_MIN_SUM))
    def _():
        def body(j, carry):
            cols = pl.ds(pl.multiple_of(j * tq, tq), tq)
            tile(cols, lambda st: jnp.max(st, axis=0, keepdims=True))
            return carry
        lax.fori_loop(0, nq, body, 0)


def _attn_call(q, k, vt, qsq, ksq, tq):
    b, s, _ = q.shape
    norm_spec = pl.BlockSpec((1, 1, 1, s), lambda bi, h: (bi, h, 0, 0))
    return pl.pallas_call(
        functools.partial(_attn_kernel, tq=tq),
        grid=(b, N_HEADS),
        in_specs=[pl.BlockSpec((1, s, QK_PAD), lambda bi, h: (bi, 0, h)),
                  pl.BlockSpec((1, s, QK_PAD), lambda bi, h: (bi, 0, h)),
                  pl.BlockSpec((1, V_AUG, s), lambda bi, h: (bi, h, 0)),
                  norm_spec, norm_spec],
        out_specs=pl.BlockSpec((1, s, V_HEAD), lambda bi, h: (bi, 0, h)),
        out_shape=jax.ShapeDtypeStruct((b, s, N_HEADS * V_HEAD), BF16),
        compiler_params=_params("parallel", "parallel"),
        name="attn",
    )(q, k, vt, qsq.reshape(b, N_HEADS, 1, s), ksq.reshape(b, N_HEADS, 1, s))


def _merge_kernel(a_ref, f_ref, g_ref, x_ref, gt_ref, gpost_ref, wao_ref, wfo_ref, wout_ref, o_ref):
    branch_a = _dot(a_ref[0], wao_ref[...])
    branch_b = _dot(f_ref[0], wfo_ref[...])
    gate_a = g_ref[0, :, :D_MODEL].astype(F32)
    gate_b = g_ref[0, :, D_MODEL:].astype(F32)
    merged = jax.nn.sigmoid(gate_a) * branch_a + jax.nn.sigmoid(gate_b) * branch_b
    y = _dot(merged.astype(BF16), wout_ref[...])
    o_ref[0] = x_ref[0] + gt_ref[0] * _rms(y, gpost_ref[...])


def _merge_call(attn, four, gates, x, gt, lw, tm):
    b, s, _ = x.shape
    row = lambda bi, i: (bi, i, 0)
    weights = [lw["g_mix_post"], lw["w_attn_o"], lw["w_four"], lw["w_out"]]
    return pl.pallas_call(
        _merge_kernel,
        grid=(b, s // tm),
        in_specs=[pl.BlockSpec((1, tm, N_HEADS * V_HEAD), row),
                  pl.BlockSpec((1, tm, FOURIER_WIDTH), row),
                  pl.BlockSpec((1, tm, 2 * D_MODEL), row),
                  pl.BlockSpec((1, tm, D_MODEL), row),
                  pl.BlockSpec((1, 1, D_MODEL), lambda bi, i: (bi, 0, 0))]
                 + [_const_spec(w.shape) for w in weights],
        out_specs=pl.BlockSpec((1, tm, D_MODEL), row),
        out_shape=jax.ShapeDtypeStruct(x.shape, F32),
        compiler_params=_params("parallel", "parallel"),
        name="merge",
    )(attn, four, gates, x, gt, *weights)


FFN_CHUNK = 2 * LANES
assert D_FF % FFN_CHUNK == 0
GELU_C = math.sqrt(2.0 / math.pi)


def _gelu_tanh(a):
    return 0.5 * a * (1.0 + jnp.tanh(a * (GELU_C + (GELU_C * 0.044715) * (a * a))))


def _ffn_kernel(xp_ref, x_ref, xn_ref, sh_ref, sc_ref, gt_ref, gpre_ref, gpost_ref,
                wua_ref, wub_ref, wca_ref, wcb_ref, bca_ref, bcb_ref, wd_ref, o_ref, act_ref):
    i = pl.program_id(1)
    last = pl.num_programs(1) - 1
    tm = x_ref.shape[1]
    te = tm + 2 * SUBLANES
    x = x_ref[0]
    xe = jnp.concatenate([xp_ref[0], x, xn_ref[0]], axis=0)
    he = _rms(xe, gpre_ref[...]) * (1.0 + sc_ref[0]) + sh_ref[0]
    r = lax.broadcasted_iota(jnp.int32, (te, 1), 0)
    inside = jnp.logical_and(jnp.logical_or(r >= SUBLANES, i > 0),
                             jnp.logical_or(r < tm + SUBLANES, i < last))
    he = jnp.where(inside, he, 0.0).astype(BF16)

    def conv(u, w_ref, b_ref, cs):
        prev = pltpu.roll(u, 1, 0)[SUBLANES:SUBLANES + tm]
        nxt = pltpu.roll(u, te - 1, 0)[SUBLANES:SUBLANES + tm]
        mid = u[SUBLANES:SUBLANES + tm]
        return prev * w_ref[0:1, cs] + mid * w_ref[1:2, cs] + nxt * w_ref[2:3, cs] + b_ref[:, cs]

    for c in range(D_FF // FFN_CHUNK):
        cs = slice(c * FFN_CHUNK, (c + 1) * FFN_CHUNK)
        a = conv(_dot(he, wua_ref[:, cs]), wca_ref, bca_ref, cs)
        g = conv(_dot(he, wub_ref[:, cs]), wcb_ref, bcb_ref, cs)
        act_ref[:, cs] = (_gelu_tanh(a) * g).astype(BF16)

    y = _dot(act_ref[...], wd_ref[...])
    o_ref[0] = x + gt_ref[0] * _rms(y, gpost_ref[...])


def _ffn_call(x, sh, sc, gt, lw, tm):
    b, s, _ = x.shape
    nb = tm // SUBLANES
    row = lambda bi, i: (bi, i, 0)
    vec = lambda bi, i: (bi, 0, 0)
    weights = [lw["g_ffn_pre"], lw["g_ffn_post"], lw["w_up_a"], lw["w_up_b"],
               lw["w_conv_a"], lw["w_conv_b"], lw["b_conv_a"], lw["b_conv_b"], lw["w_down"]]
    return pl.pallas_call(
        _ffn_kernel,
        grid=(b, s // tm),
        in_specs=[pl.BlockSpec((1, SUBLANES, D_MODEL), lambda bi, i: (bi, jnp.maximum(i * nb - 1, 0), 0)),
                  pl.BlockSpec((1, tm, D_MODEL), row),
                  pl.BlockSpec((1, SUBLANES, D_MODEL),
                               lambda bi, i: (bi, jnp.minimum((i + 1) * nb, s // SUBLANES - 1), 0)),
                  pl.BlockSpec((1, 1, D_MODEL), vec),
                  pl.BlockSpec((1, 1, D_MODEL), vec),
                  pl.BlockSpec((1, 1, D_MODEL), vec)]
                 + [_const_spec(w.shape) for w in weights],
        out_specs=pl.BlockSpec((1, tm, D_MODEL), row),
        out_shape=jax.ShapeDtypeStruct(x.shape, F32),
        scratch_shapes=[pltpu.VMEM((tm, D_FF), BF16)],
        compiler_params=_params("parallel", "parallel"),
        name="ffn",
    )(x, x, x, sh, sc, gt, *weights)


def _rope_tables(s):
    half = QK_ROPE // 2
    inv = 1.0 / (ROPE_BASE ** (jnp.arange(half, dtype=F32) / half))
    ang = jnp.arange(s, dtype=F32)[:, None] * inv[None, :]
    cos, sin = jnp.cos(ang), jnp.sin(ang)
    zero = jnp.zeros((s, LANES - QK_ROPE), F32)
    return (jnp.concatenate([cos, cos, zero], axis=1),
            jnp.concatenate([-sin, sin, zero], axis=1))


def _dft_cos_sin(n):
    idx = jnp.arange(n, dtype=jnp.int32)
    ang = ((idx[:, None] * idx[None, :]) % n).astype(F32) * (2.0 * math.pi / n)
    norm = n ** -0.5
    return jnp.cos(ang) * norm, jnp.sin(ang) * norm


def _fourier_tables(s):
    n1 = s // DFT_N2
    rows = BF16_SUBLANES
    new = jnp.arange(DFT_PB, dtype=jnp.int32)
    old = DFT_N2 * (new % rows) + new // rows
    perm = (old[:, None] == jnp.arange(DFT_PB, dtype=jnp.int32)[None, :]).astype(BF16)
    cc, sc_ = _dft_cos_sin(FOURIER_GROUP)
    wcs = jnp.concatenate([cc, sc_], axis=1).astype(BF16)
    j1 = jnp.arange(n1, dtype=jnp.int32)[None, :, None]
    s1 = jnp.arange(n1, dtype=jnp.int32)[None, None, :]
    s2 = jnp.arange(DFT_N2, dtype=jnp.int32)[:, None, None]
    ang = ((j1 * (DFT_N2 * s1 + s2)) % s).astype(F32) * (2.0 * math.pi / s)
    c1, sn1 = jnp.cos(ang) * s ** -0.5, jnp.sin(ang) * s ** -0.5
    w1 = jnp.concatenate([jnp.concatenate([c1, -sn1], axis=2),
                          jnp.concatenate([sn1, c1], axis=2)], axis=1).astype(BF16)
    j2 = jnp.arange(DFT_N2, dtype=jnp.int32)
    ang2 = ((j2[:, None] * j2[None, :]) % DFT_N2).astype(F32) * (2.0 * math.pi / DFT_N2)
    eye = jnp.eye(DFT_JG, dtype=F32)
    bd = jnp.concatenate([jnp.kron(jnp.cos(ang2), eye), jnp.kron(-jnp.sin(ang2), eye)], axis=1).astype(BF16)
    return perm, wcs, w1, bd


def _layer_weights(l, g_mix_pre, g_mix_post, w_in, g_q, w_q_b, g_kv, w_kv_b, w_attn_o, w_four, w_out,
                   g_ffn_pre, g_ffn_post, w_up, w_conv, b_conv, w_down):
    wi = w_in[l]
    wqb = w_q_b[l].reshape(Q_LORA, N_HEADS, QK_HEAD)
    wkvb = w_kv_b[l].reshape(KV_LORA, N_HEADS, QK_NOPE + V_HEAD)
    rope_pad = ((0, 0), (0, 0), (0, LANES - QK_ROPE))
    return {
        "g_mix_pre": g_mix_pre[l].reshape(1, D_MODEL),
        "g_mix_post": g_mix_post[l].reshape(1, D_MODEL),
        "w_qa": wi[:, :OFF_KV].astype(BF16),
        "w_kva": wi[:, OFF_KV:OFF_KR].astype(BF16),
        "w_kr": jnp.pad(wi[:, OFF_KR:OFF_F], ((0, 0), (0, LANES - QK_ROPE))).astype(BF16),
        "w_f": wi[:, OFF_F:OFF_G].astype(BF16),
        "w_g": wi[:, OFF_G:].astype(BF16),
        "g_q": g_q[l].reshape(1, Q_LORA),
        "g_kv": g_kv[l].reshape(1, KV_LORA),
        "w_qn": wqb[:, :, :QK_NOPE].reshape(Q_LORA, N_HEADS * QK_NOPE).astype(BF16),
        "w_qr": jnp.pad(wqb[:, :, QK_NOPE:], rope_pad).reshape(Q_LORA, N_HEADS * LANES).astype(BF16),
        "w_kb": wkvb[:, :, :QK_NOPE].reshape(KV_LORA, N_HEADS * QK_NOPE).astype(BF16),
        "w_vbt": wkvb[:, :, QK_NOPE:].reshape(KV_LORA, N_HEADS * V_HEAD).T.astype(BF16),
        "w_attn_o": w_attn_o[l].astype(BF16),
        "w_four": w_four[l].astype(BF16),
        "w_out": w_out[l].astype(BF16),
        "g_ffn_pre": g_ffn_pre[l].reshape(1, D_MODEL),
        "g_ffn_post": g_ffn_post[l].reshape(1, D_MODEL),
        "w_up_a": w_up[l][:, :D_FF].astype(BF16),
        "w_up_b": w_up[l][:, D_FF:].astype(BF16),
        "w_conv_a": w_conv[l][:, :D_FF],
        "w_conv_b": w_conv[l][:, D_FF:],
        "b_conv_a": b_conv[l][:D_FF].reshape(1, D_FF),
        "b_conv_b": b_conv[l][D_FF:].reshape(1, D_FF),
        "w_down": w_down[l].astype(BF16),
    }


def _tile(s, want):
    t = min(s, want)
    assert s % t == 0
    return t


def _trunk(x, c, w_ada, b_ada, layers):
    b, s, _ = x.shape
    tm = _tile(s, 512)
    tq = _tile(s, 512)
    assert s % DFT_PB == 0 and (s // DFT_N2) % DFT_JG == 0
    cos_t, sin_t = _rope_tables(s)
    four_tables = _fourier_tables(s)
    mods = _ada_mods(c, w_ada, b_ada).reshape(DEPTH, b, N_ADA, 1, D_MODEL)
    for l, lw in enumerate(layers):
        sh1, sc1, gt1, sh2, sc2, gt2 = [mods[l, :, j] for j in range(N_ADA)]
        q, k, vt, u, gates, qsq, ksq = _pre_call(x, sh1, sc1, lw, cos_t, sin_t, tm)
        four = _four_call(u, four_tables)
        attn = _attn_call(q, k, vt, qsq, ksq, tq)
        x = _merge_call(attn, four, gates, x, gt1, lw, tm)
        x = _ffn_call(x, sh2, sc2, gt2, lw, tm)
    return x


def kernel(x_prompt, x_sample, c_prompt, c_sample, w_ada, b_ada, g_mix_pre, g_mix_post, w_in, g_q, w_q_b,
           g_kv, w_kv_b, w_attn_o, w_four, w_out, g_ffn_pre, g_ffn_post, w_up, w_conv, b_conv, w_down):
    layers = [_layer_weights(l, g_mix_pre, g_mix_post, w_in, g_q, w_q_b, g_kv, w_kv_b, w_attn_o, w_four,
                             w_out, g_ffn_pre, g_ffn_post, w_up, w_conv, b_conv, w_down)
              for l in range(DEPTH)]
    y_prompt = _trunk(x_prompt, c_prompt, w_ada, b_ada, layers)
    y_sample = _trunk(x_sample, c_sample, w_ada, b_ada, layers)
    return (y_prompt, y_sample)
```

```python
import functools
import math

import jax
import jax.numpy as jnp
from jax import lax
from jax.experimental import pallas as pl
from jax.experimental.pallas import tpu as pltpu

D_MODEL = 1024
DEPTH = 4
N_HEADS = 8
QK_NOPE = 128
QK_ROPE = 64
V_HEAD = 128
Q_LORA = 512
KV_LORA = 256
QK_HEAD = QK_NOPE + QK_ROPE
ROPE_BASE = 10000.0
N_FOURIER_GROUPS = 4
FOURIER_GROUP = 128
FOURIER_WIDTH = N_FOURIER_GROUPS * FOURIER_GROUP
D_FF = 2816
EPS = 1e-6
N_ADA = 6
OFF_KV = Q_LORA
OFF_KR = OFF_KV + KV_LORA
OFF_F = OFF_KR + QK_ROPE
OFF_G = OFF_F + FOURIER_WIDTH

LANES = 128
SUBLANES = 8
QK_PAD = 2 * LANES
BF16_SUBLANES = 16
V_AUG = V_HEAD + BF16_SUBLANES
VMEM_LIMIT_BYTES = 60 * 1024 * 1024

BF16 = jnp.bfloat16
F32 = jnp.float32


def _dot(a, b):
    return jnp.dot(a, b, preferred_element_type=F32)


def _dot_nt(a, b):
    return lax.dot_general(a, b, (((1,), (1,)), ((), ())), preferred_element_type=F32)


def _rms(x, g):
    return x * lax.rsqrt(jnp.mean(x * x, axis=-1, keepdims=True) + EPS) * g


def _const_spec(shape):
    nd = len(shape)
    return pl.BlockSpec(shape, lambda *_: (0,) * nd, pipeline_mode=pl.Buffered(1))


def _params(*sem):
    return pltpu.CompilerParams(dimension_semantics=sem, vmem_limit_bytes=VMEM_LIMIT_BYTES)


def _ada_kernel(c_ref, w_ref, b_ref, o_ref):
    c = c_ref[...]
    s = c * jax.nn.sigmoid(c)
    o_ref[0] = jnp.dot(s, w_ref[0], preferred_element_type=F32,
                       precision=lax.Precision.HIGHEST) + b_ref[0]


def _ada_mods(c, w_ada, b_ada):
    b = c.shape[0]
    bp = -(-b // SUBLANES) * SUBLANES
    cp = jnp.pad(c, ((0, bp - b), (0, 0)))
    out = pl.pallas_call(
        _ada_kernel,
        grid=(DEPTH, N_ADA),
        in_specs=[
            pl.BlockSpec((bp, D_MODEL), lambda l, j: (0, 0)),
            pl.BlockSpec((1, D_MODEL, D_MODEL), lambda l, j: (l, 0, j)),
            pl.BlockSpec((1, 1, D_MODEL), lambda l, j: (l, 0, j)),
        ],
        out_specs=pl.BlockSpec((1, bp, D_MODEL), lambda l, j: (l, 0, j)),
        out_shape=jax.ShapeDtypeStruct((DEPTH, bp, N_ADA * D_MODEL), F32),
        compiler_params=_params("arbitrary", "arbitrary"),
        name="ada",
    )(cp, w_ada, b_ada.reshape(DEPTH, 1, N_ADA * D_MODEL))
    return out[:, :b]


def _pre_kernel(x_ref, sh_ref, sc_ref, gpre_ref, wqa_ref, wkva_ref, wkr_ref, wf_ref, wg_ref,
                gq_ref, gkv_ref, wqn_ref, wqr_ref, wkb_ref, wvbt_ref, hsum_ref, cos_ref, sin_ref,
                q_ref, k_ref, vt_ref, u_ref, g_ref, qsq_ref, ksq_ref):
    tm = x_ref.shape[1]
    x = x_ref[0]
    h = _rms(x, gpre_ref[...]) * (1.0 + sc_ref[0]) + sh_ref[0]
    hb = h.astype(BF16)

    zq = _dot(hb, wqa_ref[...])
    zkv = _dot(hb, wkva_ref[...])
    zkr = _dot(hb, wkr_ref[...])
    cq = _rms(zq, gq_ref[...]).astype(BF16)
    ckv = _rms(zkv, gkv_ref[...]).astype(BF16)

    cos = cos_ref[...]
    sin = sin_ref[...]
    lane = lax.broadcasted_iota(jnp.int32, (tm, LANES), 1)
    low_half = (lane % QK_ROPE) < (QK_ROPE // 2)

    def rope(t):
        rot = jnp.where(low_half, pltpu.roll(t, LANES - QK_ROPE // 2, 1), pltpu.roll(t, QK_ROPE // 2, 1))
        return t * cos + rot * sin

    scale = QK_HEAD ** -0.5 * math.log2(math.e)
    kr = rope(zkr).astype(BF16)
    qn = _dot(cq, wqn_ref[...])
    qr = _dot(cq, wqr_ref[...])
    kn = _dot(ckv, wkb_ref[...])
    u_ref[0] = _dot(hb, wf_ref[...]).astype(BF16)
    g_ref[0] = _dot(hb, wg_ref[...]).astype(BF16)
    vt = _dot_nt(wvbt_ref[...], ckv)
    extra = lax.broadcasted_iota(jnp.int32, (V_AUG - V_HEAD, tm), 0)
    ones_row = jnp.where(extra == 0, 1.0, 0.0).astype(BF16)
    for hd in range(N_HEADS):
        vt_ref[0, hd * V_AUG:hd * V_AUG + V_HEAD, :] = vt[hd * V_HEAD:(hd + 1) * V_HEAD].astype(BF16)
        vt_ref[0, hd * V_AUG + V_HEAD:(hd + 1) * V_AUG, :] = ones_row

    def sq(t):
        t = t.astype(F32)
        return t * t

    kr_sq = sq(kr)
    q_sq, k_sq = [], []
    for hd in range(N_HEADS):
        lo = hd * QK_PAD
        sl = slice(hd * LANES, (hd + 1) * LANES)
        q_nope = (qn[:, sl] * scale).astype(BF16)
        q_rope = (rope(qr[:, sl]) * scale).astype(BF16)
        k_nope = kn[:, sl].astype(BF16)
        q_ref[0, :, lo:lo + LANES] = q_nope
        q_ref[0, :, lo + LANES:lo + QK_PAD] = q_rope
        k_ref[0, :, lo:lo + LANES] = k_nope
        k_ref[0, :, lo + LANES:lo + QK_PAD] = kr
        q_sq.append((sq(q_nope) + sq(q_rope)).astype(BF16))
        k_sq.append((sq(k_nope) + kr_sq).astype(BF16))
    qsq_ref[0] = _dot_nt(hsum_ref[...], jnp.concatenate(q_sq, axis=1))[:N_HEADS]
    ksq_ref[0] = _dot_nt(hsum_ref[...], jnp.concatenate(k_sq, axis=1))[:N_HEADS]


def _head_sum_matrix():
    r = jnp.arange(BF16_SUBLANES, dtype=jnp.int32)[:, None]
    c = jnp.arange(N_HEADS * LANES, dtype=jnp.int32)[None, :] // LANES
    return (r == c).astype(BF16)


def _pre_call(x, sh, sc, lw, cos_t, sin_t, tm):
    b, s, _ = x.shape
    row = lambda bi, i: (bi, i, 0)
    vec = lambda bi, i: (bi, 0, 0)
    weights = [lw["g_mix_pre"], lw["w_qa"], lw["w_kva"], lw["w_kr"], lw["w_f"], lw["w_g"],
               lw["g_q"], lw["g_kv"], lw["w_qn"], lw["w_qr"], lw["w_kb"], lw["w_vbt"], _head_sum_matrix()]
    return pl.pallas_call(
        _pre_kernel,
        grid=(b, s // tm),
        in_specs=[pl.BlockSpec((1, tm, D_MODEL), row),
                  pl.BlockSpec((1, 1, D_MODEL), vec),
                  pl.BlockSpec((1, 1, D_MODEL), vec)]
                 + [_const_spec(w.shape) for w in weights]
                 + [pl.BlockSpec((tm, LANES), lambda bi, i: (i, 0)),
                    pl.BlockSpec((tm, LANES), lambda bi, i: (i, 0))],
        out_specs=[pl.BlockSpec((1, tm, N_HEADS * QK_PAD), row),
                   pl.BlockSpec((1, tm, N_HEADS * QK_PAD), row),
                   pl.BlockSpec((1, N_HEADS * V_AUG, tm), lambda bi, i: (bi, 0, i)),
                   pl.BlockSpec((1, tm, FOURIER_WIDTH), row),
                   pl.BlockSpec((1, tm, 2 * D_MODEL), row),
                   pl.BlockSpec((1, N_HEADS, tm), lambda bi, i: (bi, 0, i)),
                   pl.BlockSpec((1, N_HEADS, tm), lambda bi, i: (bi, 0, i))],
        out_shape=[jax.ShapeDtypeStruct((b, s, N_HEADS * QK_PAD), BF16),
                   jax.ShapeDtypeStruct((b, s, N_HEADS * QK_PAD), BF16),
                   jax.ShapeDtypeStruct((b, N_HEADS * V_AUG, s), BF16),
                   jax.ShapeDtypeStruct((b, s, FOURIER_WIDTH), BF16),
                   jax.ShapeDtypeStruct((b, s, 2 * D_MODEL), BF16),
                   jax.ShapeDtypeStruct((b, N_HEADS, s), F32),
                   jax.ShapeDtypeStruct((b, N_HEADS, s), F32)],
        compiler_params=_params("parallel", "parallel"),
        name="pre",
    )(x, sh, sc, *weights, cos_t, sin_t)


DFT_N2 = 16
DFT_JG = BF16_SUBLANES
DFT_PB = DFT_N2 * BF16_SUBLANES


def _four_kernel(u_ref, perm_ref, wcs_ref, w1_ref, bd_ref, o_ref, ab_ref, z_ref):
    s = u_ref.shape[1]
    n1 = s // DFT_N2
    rows = BF16_SUBLANES
    for blk in range(s // DFT_PB):
        ub = u_ref[0, blk * DFT_PB:(blk + 1) * DFT_PB, :]
        up = _dot(perm_ref[...], ub).astype(BF16)
        for g in range(N_FOURIER_GROUPS):
            sl = slice(g * FOURIER_GROUP, (g + 1) * FOURIER_GROUP)
            ab = _dot(up[:, sl], wcs_ref[...]).astype(BF16)
            for s2 in range(DFT_N2):
                src = slice(s2 * rows, (s2 + 1) * rows)
                dst = slice(blk * rows, (blk + 1) * rows)
                ab_ref[s2, dst, sl] = ab[src, :FOURIER_GROUP]
                ab_ref[s2, n1 + blk * rows:n1 + (blk + 1) * rows, sl] = ab[src, FOURIER_GROUP:]
    kg = 2 * DFT_N2 * DFT_JG
    for s2 in range(DFT_N2):
        z = _dot(w1_ref[s2], ab_ref[s2]).astype(BF16)
        for grp in range(n1 // DFT_JG):
            for r in range(2):
                dst = grp * kg + r * DFT_N2 * DFT_JG + s2 * DFT_JG
                z_ref[dst:dst + DFT_JG, :] = z[r * n1 + grp * DFT_JG:r * n1 + (grp + 1) * DFT_JG, :]
    for grp in range(n1 // DFT_JG):
        y = _dot(bd_ref[...], z_ref[grp * kg:(grp + 1) * kg, :]).astype(BF16)
        for j2 in range(DFT_N2):
            dst = n1 * j2 + grp * DFT_JG
            o_ref[0, dst:dst + DFT_JG, :] = y[j2 * DFT_JG:(j2 + 1) * DFT_JG, :]


def _four_call(u, tables):
    b, s, _ = u.shape
    n1 = s // DFT_N2
    return pl.pallas_call(
        _four_kernel,
        grid=(b,),
        in_specs=[pl.BlockSpec((1, s, FOURIER_WIDTH), lambda bi: (bi, 0, 0))]
                 + [_const_spec(t.shape) for t in tables],
        out_specs=pl.BlockSpec((1, s, FOURIER_WIDTH), lambda bi: (bi, 0, 0)),
        out_shape=jax.ShapeDtypeStruct((b, s, FOURIER_WIDTH), BF16),
        scratch_shapes=[pltpu.VMEM((DFT_N2, 2 * n1, FOURIER_WIDTH), BF16),
                        pltpu.VMEM((2 * s, FOURIER_WIDTH), BF16)],
        compiler_params=_params("parallel"),
        name="four",
    )(u, *tables)


ATTN_MIN_SUM = 2.0 ** -64
ATTN_BOUND_SLACK = 1.0 + 2.0 ** -6


def _attn_kernel(q_ref, k_ref, vt_ref, qsq_ref, ksq_ref, o_ref, *, tq):
    nq = q_ref.shape[1] // tq

    def tile(cols, m_of):
        st = _dot_nt(k_ref[0], q_ref[0, cols, :])
        p = jnp.exp2(st - m_of(st))
        ot = _dot(vt_ref[0], p.astype(BF16))
        l = ot[V_HEAD:V_HEAD + 1]
        o_ref[0, cols, :] = (ot[:V_HEAD] * (1.0 / l)).T.astype(BF16)
        return l

    k_max = jnp.sqrt(jnp.max(ksq_ref[0, 0], axis=-1, keepdims=True)) * ATTN_BOUND_SLACK
    l_min = None
    for j in range(nq):
        cols = slice(j * tq, (j + 1) * tq)
        bound = jnp.sqrt(qsq_ref[0, 0, :, cols]) * k_max
        l = tile(cols, lambda st: bound)
        l_min = l if l_min is None else jnp.minimum(l_min, l)

    @pl.when(jnp.logical_not(jnp.min(l_min) >= ATTN_MIN_SUM))
    def _():
        def body(j, carry):
            cols = pl.ds(pl.multiple_of(j * tq, tq), tq)
            tile(cols, lambda st: jnp.max(st, axis=0, keepdims=True))
            return carry
        lax.fori_loop(0, nq, body, 0)


def _attn_call(q, k, vt, qsq, ksq, tq):
    b, s, _ = q.shape
    norm_spec = pl.BlockSpec((1, 1, 1, s), lambda bi, h: (bi, h, 0, 0))
    return pl.pallas_call(
        functools.partial(_attn_kernel, tq=tq),
        grid=(b, N_HEADS),
        in_specs=[pl.BlockSpec((1, s, QK_PAD), lambda bi, h: (bi, 0, h)),
                  pl.BlockSpec((1, s, QK_PAD), lambda bi, h: (bi, 0, h)),
                  pl.BlockSpec((1, V_AUG, s), lambda bi, h: (bi, h, 0)),
                  norm_spec, norm_spec],
        out_specs=pl.BlockSpec((1, s, V_HEAD), lambda bi, h: (bi, 0, h)),
        out_shape=jax.ShapeDtypeStruct((b, s, N_HEADS * V_HEAD), BF16),
        compiler_params=_params("parallel", "parallel"),
        name="attn",
    )(q, k, vt, qsq.reshape(b, N_HEADS, 1, s), ksq.reshape(b, N_HEADS, 1, s))


def _merge_kernel(a_ref, f_ref, g_ref, x_ref, gt_ref, gpost_ref, wao_ref, wfo_ref, wout_ref, o_ref):
    half = x_ref.shape[1] // 2
    halves = (slice(0, half), slice(half, 2 * half))
    branches = [(_dot(a_ref[0, rows, :], wao_ref[...]), _dot(f_ref[0, rows, :], wfo_ref[...])) for rows in halves]
    ys = []
    for rows, (branch_a, branch_b) in zip(halves, branches):
        gate_a = g_ref[0, rows, :D_MODEL].astype(F32)
        gate_b = g_ref[0, rows, D_MODEL:].astype(F32)
        merged = jax.nn.sigmoid(gate_a) * branch_a + jax.nn.sigmoid(gate_b) * branch_b
        ys.append(_dot(merged.astype(BF16), wout_ref[...]))
    for rows, y in zip(halves, ys):
        o_ref[0, rows, :] = x_ref[0, rows, :] + gt_ref[0] * _rms(y, gpost_ref[...])


def _merge_call(attn, four, gates, x, gt, lw, tm):
    b, s, _ = x.shape
    row = lambda bi, i: (bi, i, 0)
    weights = [lw["g_mix_post"], lw["w_attn_o"], lw["w_four"], lw["w_out"]]
    return pl.pallas_call(
        _merge_kernel,
        grid=(b, s // tm),
        in_specs=[pl.BlockSpec((1, tm, N_HEADS * V_HEAD), row),
                  pl.BlockSpec((1, tm, FOURIER_WIDTH), row),
                  pl.BlockSpec((1, tm, 2 * D_MODEL), row),
                  pl.BlockSpec((1, tm, D_MODEL), row),
                  pl.BlockSpec((1, 1, D_MODEL), lambda bi, i: (bi, 0, 0))]
                 + [_const_spec(w.shape) for w in weights],
        out_specs=pl.BlockSpec((1, tm, D_MODEL), row),
        out_shape=jax.ShapeDtypeStruct(x.shape, F32),
        compiler_params=_params("parallel", "parallel"),
        name="merge",
    )(attn, four, gates, x, gt, *weights)


FFN_CHUNK = 2 * LANES
assert D_FF % FFN_CHUNK == 0
GELU_C = math.sqrt(2.0 / math.pi)


def _gelu_tanh(a):
    return 0.5 * a * (1.0 + jnp.tanh(a * (GELU_C + (GELU_C * 0.044715) * (a * a))))


def _ffn_kernel(xp_ref, x_ref, xn_ref, sh_ref, sc_ref, gt_ref, gpre_ref, gpost_ref,
                wua_ref, wub_ref, wca_ref, wcb_ref, bca_ref, bcb_ref, wd_ref, o_ref, act_ref):
    i = pl.program_id(1)
    last = pl.num_programs(1) - 1
    tm = x_ref.shape[1]
    te = tm + 2 * SUBLANES
    x = x_ref[0]
    xe = jnp.concatenate([xp_ref[0], x, xn_ref[0]], axis=0)
    he = _rms(xe, gpre_ref[...]) * (1.0 + sc_ref[0]) + sh_ref[0]
    r = lax.broadcasted_iota(jnp.int32, (te, 1), 0)
    inside = jnp.logical_and(jnp.logical_or(r >= SUBLANES, i > 0),
                             jnp.logical_or(r < tm + SUBLANES, i < last))
    he = jnp.where(inside, he, 0.0).astype(BF16)

    def conv(u, w_ref, b_ref, cs):
        prev = pltpu.roll(u, 1, 0)[SUBLANES:SUBLANES + tm]
        nxt = pltpu.roll(u, te - 1, 0)[SUBLANES:SUBLANES + tm]
        mid = u[SUBLANES:SUBLANES + tm]
        return prev * w_ref[0:1, cs] + mid * w_ref[1:2, cs] + nxt * w_ref[2:3, cs] + b_ref[:, cs]

    for c in range(D_FF // FFN_CHUNK):
        cs = slice(c * FFN_CHUNK, (c + 1) * FFN_CHUNK)
        a = conv(_dot(he, wua_ref[:, cs]), wca_ref, bca_ref, cs)
        g = conv(_dot(he, wub_ref[:, cs]), wcb_ref, bcb_ref, cs)
        act_ref[:, cs] = (_gelu_tanh(a) * g).astype(BF16)

    y = _dot(act_ref[...], wd_ref[...])
    o_ref[0] = x + gt_ref[0] * _rms(y, gpost_ref[...])


def _ffn_call(x, sh, sc, gt, lw, tm):
    b, s, _ = x.shape
    nb = tm // SUBLANES
    row = lambda bi, i: (bi, i, 0)
    vec = lambda bi, i: (bi, 0, 0)
    weights = [lw["g_ffn_pre"], lw["g_ffn_post"], lw["w_up_a"], lw["w_up_b"],
               lw["w_conv_a"], lw["w_conv_b"], lw["b_conv_a"], lw["b_conv_b"], lw["w_down"]]
    return pl.pallas_call(
        _ffn_kernel,
        grid=(b, s // tm),
        in_specs=[pl.BlockSpec((1, SUBLANES, D_MODEL), lambda bi, i: (bi, jnp.maximum(i * nb - 1, 0), 0)),
                  pl.BlockSpec((1, tm, D_MODEL), row),
                  pl.BlockSpec((1, SUBLANES, D_MODEL),
                               lambda bi, i: (bi, jnp.minimum((i + 1) * nb, s // SUBLANES - 1), 0)),
                  pl.BlockSpec((1, 1, D_MODEL), vec),
                  pl.BlockSpec((1, 1, D_MODEL), vec),
                  pl.BlockSpec((1, 1, D_MODEL), vec)]
                 + [_const_spec(w.shape) for w in weights],
        out_specs=pl.BlockSpec((1, tm, D_MODEL), row),
        out_shape=jax.ShapeDtypeStruct(x.shape, F32),
        scratch_shapes=[pltpu.VMEM((tm, D_FF), BF16)],
        compiler_params=_params("parallel", "parallel"),
        name="ffn",
    )(x, x, x, sh, sc, gt, *weights)


def _rope_tables(s):
    half = QK_ROPE // 2
    inv = 1.0 / (ROPE_BASE ** (jnp.arange(half, dtype=F32) / half))
    ang = jnp.arange(s, dtype=F32)[:, None] * inv[None, :]
    cos, sin = jnp.cos(ang), jnp.sin(ang)
    zero = jnp.zeros((s, LANES - QK_ROPE), F32)
    return (jnp.concatenate([cos, cos, zero], axis=1),
            jnp.concatenate([-sin, sin, zero], axis=1))


def _dft_cos_sin(n):
    idx = jnp.arange(n, dtype=jnp.int32)
    ang = ((idx[:, None] * idx[None, :]) % n).astype(F32) * (2.0 * math.pi / n)
    norm = n ** -0.5
    return jnp.cos(ang) * norm, jnp.sin(ang) * norm


def _fourier_tables(s):
    n1 = s // DFT_N2
    rows = BF16_SUBLANES
    new = jnp.arange(DFT_PB, dtype=jnp.int32)
    old = DFT_N2 * (new % rows) + new // rows
    perm = (old[:, None] == jnp.arange(DFT_PB, dtype=jnp.int32)[None, :]).astype(BF16)
    cc, sc_ = _dft_cos_sin(FOURIER_GROUP)
    wcs = jnp.concatenate([cc, sc_], axis=1).astype(BF16)
    j1 = jnp.arange(n1, dtype=jnp.int32)[None, :, None]
    s1 = jnp.arange(n1, dtype=jnp.int32)[None, None, :]
    s2 = jnp.arange(DFT_N2, dtype=jnp.int32)[:, None, None]
    ang = ((j1 * (DFT_N2 * s1 + s2)) % s).astype(F32) * (2.0 * math.pi / s)
    c1, sn1 = jnp.cos(ang) * s ** -0.5, jnp.sin(ang) * s ** -0.5
    w1 = jnp.concatenate([jnp.concatenate([c1, -sn1], axis=2),
                          jnp.concatenate([sn1, c1], axis=2)], axis=1).astype(BF16)
    j2 = jnp.arange(DFT_N2, dtype=jnp.int32)
    ang2 = ((j2[:, None] * j2[None, :]) % DFT_N2).astype(F32) * (2.0 * math.pi / DFT_N2)
    eye = jnp.eye(DFT_JG, dtype=F32)
    bd = jnp.concatenate([jnp.kron(jnp.cos(ang2), eye), jnp.kron(-jnp.sin(ang2), eye)], axis=1).astype(BF16)
    return perm, wcs, w1, bd


def _layer_weights(l, g_mix_pre, g_mix_post, w_in, g_q, w_q_b, g_kv, w_kv_b, w_attn_o, w_four, w_out,
                   g_ffn_pre, g_ffn_post, w_up, w_conv, b_conv, w_down):
    wi = w_in[l]
    wqb = w_q_b[l].reshape(Q_LORA, N_HEADS, QK_HEAD)
    wkvb = w_kv_b[l].reshape(KV_LORA, N_HEADS, QK_NOPE + V_HEAD)
    rope_pad = ((0, 0), (0, 0), (0, LANES - QK_ROPE))
    return {
        "g_mix_pre": g_mix_pre[l].reshape(1, D_MODEL),
        "g_mix_post": g_mix_post[l].reshape(1, D_MODEL),
        "w_qa": wi[:, :OFF_KV].astype(BF16),
        "w_kva": wi[:, OFF_KV:OFF_KR].astype(BF16),
        "w_kr": jnp.pad(wi[:, OFF_KR:OFF_F], ((0, 0), (0, LANES - QK_ROPE))).astype(BF16),
        "w_f": wi[:, OFF_F:OFF_G].astype(BF16),
        "w_g": wi[:, OFF_G:].astype(BF16),
        "g_q": g_q[l].reshape(1, Q_LORA),
        "g_kv": g_kv[l].reshape(1, KV_LORA),
        "w_qn": wqb[:, :, :QK_NOPE].reshape(Q_LORA, N_HEADS * QK_NOPE).astype(BF16),
        "w_qr": jnp.pad(wqb[:, :, QK_NOPE:], rope_pad).reshape(Q_LORA, N_HEADS * LANES).astype(BF16),
        "w_kb": wkvb[:, :, :QK_NOPE].reshape(KV_LORA, N_HEADS * QK_NOPE).astype(BF16),
        "w_vbt": wkvb[:, :, QK_NOPE:].reshape(KV_LORA, N_HEADS * V_HEAD).T.astype(BF16),
        "w_attn_o": w_attn_o[l].astype(BF16),
        "w_four": w_four[l].astype(BF16),
        "w_out": w_out[l].astype(BF16),
        "g_ffn_pre": g_ffn_pre[l].reshape(1, D_MODEL),
        "g_ffn_post": g_ffn_post[l].reshape(1, D_MODEL),
        "w_up_a": w_up[l][:, :D_FF].astype(BF16),
        "w_up_b": w_up[l][:, D_FF:].astype(BF16),
        "w_conv_a": w_conv[l][:, :D_FF],
        "w_conv_b": w_conv[l][:, D_FF:],
        "b_conv_a": b_conv[l][:D_FF].reshape(1, D_FF),
        "b_conv_b": b_conv[l][D_FF:].reshape(1, D_FF),
        "w_down": w_down[l].astype(BF16),
    }


def _tile(s, want):
    t = min(s, want)
    assert s % t == 0
    return t


def _trunk(x, c, w_ada, b_ada, layers):
    b, s, _ = x.shape
    tm = _tile(s, 512)
    tm_big = _tile(s, 1024)
    tq = _tile(s, 512)
    assert s % DFT_PB == 0 and (s // DFT_N2) % DFT_JG == 0
    cos_t, sin_t = _rope_tables(s)
    four_tables = _fourier_tables(s)
    mods = _ada_mods(c, w_ada, b_ada).reshape(DEPTH, b, N_ADA, 1, D_MODEL)
    for l, lw in enumerate(layers):
        sh1, sc1, gt1, sh2, sc2, gt2 = [mods[l, :, j] for j in range(N_ADA)]
        q, k, vt, u, gates, qsq, ksq = _pre_call(x, sh1, sc1, lw, cos_t, sin_t, tm)
        four = _four_call(u, four_tables)
        attn = _attn_call(q, k, vt, qsq, ksq, tq)
        x = _merge_call(attn, four, gates, x, gt1, lw, tm_big)
        x = _ffn_call(x, sh2, sc2, gt2, lw, tm_big)
    return x


def kernel(x_prompt, x_sample, c_prompt, c_sample, w_ada, b_ada, g_mix_pre, g_mix_post, w_in, g_q, w_q_b,
           g_kv, w_kv_b, w_attn_o, w_four, w_out, g_ffn_pre, g_ffn_post, w_up, w_conv, b_conv, w_down):
    layers = [_layer_weights(l, g_mix_pre, g_mix_post, w_in, g_q, w_q_b, g_kv, w_kv_b, w_attn_o, w_four,
                             w_out, g_ffn_pre, g_ffn_post, w_up, w_conv, b_conv, w_down)
              for l in range(DEPTH)]
    y_prompt = _trunk(x_prompt, c_prompt, w_ada, b_ada, layers)
    y_sample = _trunk(x_sample, c_sample, w_ada, b_ada, layers)
    return (y_prompt, y_sample)
```

```python
import functools
import math

import jax
import jax.numpy as jnp
from jax import lax
from jax.experimental import pallas as pl
from jax.experimental.pallas import tpu as pltpu

D_MODEL = 1024
DEPTH = 4
N_HEADS = 8
QK_NOPE = 128
QK_ROPE = 64
V_HEAD = 128
Q_LORA = 512
KV_LORA = 256
QK_HEAD = QK_NOPE + QK_ROPE
ROPE_BASE = 10000.0
N_FOURIER_GROUPS = 4
FOURIER_GROUP = 128
FOURIER_WIDTH = N_FOURIER_GROUPS * FOURIER_GROUP
D_FF = 2816
EPS = 1e-6
N_ADA = 6
OFF_KV = Q_LORA
OFF_KR = OFF_KV + KV_LORA
OFF_F = OFF_KR + QK_ROPE
OFF_G = OFF_F + FOURIER_WIDTH

LANES = 128
SUBLANES = 8
QK_PAD = 2 * LANES
BF16_SUBLANES = 16
V_AUG = V_HEAD + BF16_SUBLANES
VMEM_LIMIT_BYTES = 60 * 1024 * 1024

BF16 = jnp.bfloat16
F32 = jnp.float32


def _dot(a, b):
    return jnp.dot(a, b, preferred_element_type=F32)


def _dot_nt(a, b):
    return lax.dot_general(a, b, (((1,), (1,)), ((), ())), preferred_element_type=F32)


def _rms(x, g):
    return x * lax.rsqrt(jnp.mean(x * x, axis=-1, keepdims=True) + EPS) * g


def _const_spec(shape):
    nd = len(shape)
    return pl.BlockSpec(shape, lambda *_: (0,) * nd, pipeline_mode=pl.Buffered(1))


def _params(*sem):
    return pltpu.CompilerParams(dimension_semantics=sem, vmem_limit_bytes=VMEM_LIMIT_BYTES)


def _ada_kernel(c_ref, w_ref, b_ref, o_ref):
    c = c_ref[...]
    s = c * jax.nn.sigmoid(c)
    o_ref[0] = jnp.dot(s, w_ref[0], preferred_element_type=F32,
                       precision=lax.Precision.HIGHEST) + b_ref[0]


def _ada_mods(c, w_ada, b_ada):
    b = c.shape[0]
    bp = -(-b // SUBLANES) * SUBLANES
    cp = jnp.pad(c, ((0, bp - b), (0, 0)))
    out = pl.pallas_call(
        _ada_kernel,
        grid=(DEPTH, N_ADA),
        in_specs=[
            pl.BlockSpec((bp, D_MODEL), lambda l, j: (0, 0)),
            pl.BlockSpec((1, D_MODEL, D_MODEL), lambda l, j: (l, 0, j)),
            pl.BlockSpec((1, 1, D_MODEL), lambda l, j: (l, 0, j)),
        ],
        out_specs=pl.BlockSpec((1, bp, D_MODEL), lambda l, j: (l, 0, j)),
        out_shape=jax.ShapeDtypeStruct((DEPTH, bp, N_ADA * D_MODEL), F32),
        compiler_params=_params("arbitrary", "arbitrary"),
        name="ada",
    )(cp, w_ada, b_ada.reshape(DEPTH, 1, N_ADA * D_MODEL))
    return out[:, :b]


def _pre_kernel(x_ref, sh_ref, sc_ref, gpre_ref, wqa_ref, wkva_ref, wkr_ref, wf_ref, wg_ref,
                gq_ref, gkv_ref, wqn_ref, wqr_ref, wkb_ref, wvbt_ref, hsum_ref, cos_ref, sin_ref,
                q_ref, k_ref, vt_ref, u_ref, g_ref, qsq_ref, ksq_ref):
    tm = x_ref.shape[1]
    x = x_ref[0]
    h = _rms(x, gpre_ref[...]) * (1.0 + sc_ref[0]) + sh_ref[0]
    hb = h.astype(BF16)

    zq = _dot(hb, wqa_ref[...])
    zkv = _dot(hb, wkva_ref[...])
    zkr = _dot(hb, wkr_ref[...])
    cq = _rms(zq, gq_ref[...]).astype(BF16)
    ckv = _rms(zkv, gkv_ref[...]).astype(BF16)

    cos = cos_ref[...]
    sin = sin_ref[...]
    lane = lax.broadcasted_iota(jnp.int32, (tm, LANES), 1)
    low_half = (lane % QK_ROPE) < (QK_ROPE // 2)

    def rope(t):
        rot = jnp.where(low_half, pltpu.roll(t, LANES - QK_ROPE // 2, 1), pltpu.roll(t, QK_ROPE // 2, 1))
        return t * cos + rot * sin

    scale = QK_HEAD ** -0.5 * math.log2(math.e)
    kr = rope(zkr).astype(BF16)
    qn = _dot(cq, wqn_ref[...])
    qr = _dot(cq, wqr_ref[...])
    kn = _dot(ckv, wkb_ref[...])
    u_ref[0] = _dot(hb, wf_ref[...]).astype(BF16)
    g_ref[0] = _dot(hb, wg_ref[...]).astype(BF16)
    vt = _dot_nt(wvbt_ref[...], ckv)
    extra = lax.broadcasted_iota(jnp.int32, (V_AUG - V_HEAD, tm), 0)
    ones_row = jnp.where(extra == 0, 1.0, 0.0).astype(BF16)
    for hd in range(N_HEADS):
        vt_ref[0, hd * V_AUG:hd * V_AUG + V_HEAD, :] = vt[hd * V_HEAD:(hd + 1) * V_HEAD].astype(BF16)
        vt_ref[0, hd * V_AUG + V_HEAD:(hd + 1) * V_AUG, :] = ones_row

    def sq(t):
        t = t.astype(F32)
        return t * t

    kr_sq = sq(kr)
    q_sq, k_sq = [], []
    for hd in range(N_HEADS):
        lo = hd * QK_PAD
        sl = slice(hd * LANES, (hd + 1) * LANES)
        q_nope = (qn[:, sl] * scale).astype(BF16)
        q_rope = (rope(qr[:, sl]) * scale).astype(BF16)
        k_nope = kn[:, sl].astype(BF16)
        q_ref[0, :, lo:lo + LANES] = q_nope
        q_ref[0, :, lo + LANES:lo + QK_PAD] = q_rope
        k_ref[0, :, lo:lo + LANES] = k_nope
        k_ref[0, :, lo + LANES:lo + QK_PAD] = kr
        q_sq.append((sq(q_nope) + sq(q_rope)).astype(BF16))
        k_sq.append((sq(k_nope) + kr_sq).astype(BF16))
    qsq_ref[0] = _dot_nt(hsum_ref[...], jnp.concatenate(q_sq, axis=1))[:N_HEADS]
    ksq_ref[0] = _dot_nt(hsum_ref[...], jnp.concatenate(k_sq, axis=1))[:N_HEADS]


def _head_sum_matrix():
    r = jnp.arange(BF16_SUBLANES, dtype=jnp.int32)[:, None]
    c = jnp.arange(N_HEADS * LANES, dtype=jnp.int32)[None, :] // LANES
    return (r == c).astype(BF16)


def _pre_call(x, sh, sc, lw, cos_t, sin_t, tm):
    b, s, _ = x.shape
    row = lambda bi, i: (bi, i, 0)
    vec = lambda bi, i: (bi, 0, 0)
    weights = [lw["g_mix_pre"], lw["w_qa"], lw["w_kva"], lw["w_kr"], lw["w_f"], lw["w_g"],
               lw["g_q"], lw["g_kv"], lw["w_qn"], lw["w_qr"], lw["w_kb"], lw["w_vbt"], _head_sum_matrix()]
    return pl.pallas_call(
        _pre_kernel,
        grid=(b, s // tm),
        in_specs=[pl.BlockSpec((1, tm, D_MODEL), row),
                  pl.BlockSpec((1, 1, D_MODEL), vec),
                  pl.BlockSpec((1, 1, D_MODEL), vec)]
                 + [_const_spec(w.shape) for w in weights]
                 + [pl.BlockSpec((tm, LANES), lambda bi, i: (i, 0)),
                    pl.BlockSpec((tm, LANES), lambda bi, i: (i, 0))],
        out_specs=[pl.BlockSpec((1, tm, N_HEADS * QK_PAD), row),
                   pl.BlockSpec((1, tm, N_HEADS * QK_PAD), row),
                   pl.BlockSpec((1, N_HEADS * V_AUG, tm), lambda bi, i: (bi, 0, i)),
                   pl.BlockSpec((1, tm, FOURIER_WIDTH), row),
                   pl.BlockSpec((1, tm, 2 * D_MODEL), row),
                   pl.BlockSpec((1, N_HEADS, tm), lambda bi, i: (bi, 0, i)),
                   pl.BlockSpec((1, N_HEADS, tm), lambda bi, i: (bi, 0, i))],
        out_shape=[jax.ShapeDtypeStruct((b, s, N_HEADS * QK_PAD), BF16),
                   jax.ShapeDtypeStruct((b, s, N_HEADS * QK_PAD), BF16),
                   jax.ShapeDtypeStruct((b, N_HEADS * V_AUG, s), BF16),
                   jax.ShapeDtypeStruct((b, s, FOURIER_WIDTH), BF16),
                   jax.ShapeDtypeStruct((b, s, 2 * D_MODEL), BF16),
                   jax.ShapeDtypeStruct((b, N_HEADS, s), F32),
                   jax.ShapeDtypeStruct((b, N_HEADS, s), F32)],
        compiler_params=_params("parallel", "parallel"),
        name="pre",
    )(x, sh, sc, *weights, cos_t, sin_t)


DFT_N2 = 16
DFT_JG = BF16_SUBLANES
DFT_PB = DFT_N2 * BF16_SUBLANES


def _four_kernel(u_ref, perm_ref, wcs_ref, w1_ref, bd_ref, o_ref, ab_ref, z_ref):
    s = u_ref.shape[1]
    n1 = s // DFT_N2
    rows = BF16_SUBLANES
    for blk in range(s // DFT_PB):
        ub = u_ref[0, blk * DFT_PB:(blk + 1) * DFT_PB, :]
        up = _dot(perm_ref[...], ub).astype(BF16)
        for g in range(N_FOURIER_GROUPS):
            sl = slice(g * FOURIER_GROUP, (g + 1) * FOURIER_GROUP)
            ab = _dot(up[:, sl], wcs_ref[...]).astype(BF16)
            for s2 in range(DFT_N2):
                src = slice(s2 * rows, (s2 + 1) * rows)
                dst = slice(blk * rows, (blk + 1) * rows)
                ab_ref[s2, dst, sl] = ab[src, :FOURIER_GROUP]
                ab_ref[s2, n1 + blk * rows:n1 + (blk + 1) * rows, sl] = ab[src, FOURIER_GROUP:]
    kg = 2 * DFT_N2 * DFT_JG
    for s2 in range(DFT_N2):
        z = _dot(w1_ref[s2], ab_ref[s2]).astype(BF16)
        for grp in range(n1 // DFT_JG):
            for r in range(2):
                dst = grp * kg + r * DFT_N2 * DFT_JG + s2 * DFT_JG
                z_ref[dst:dst + DFT_JG, :] = z[r * n1 + grp * DFT_JG:r * n1 + (grp + 1) * DFT_JG, :]
    for grp in range(n1 // DFT_JG):
        y = _dot(bd_ref[...], z_ref[grp * kg:(grp + 1) * kg, :]).astype(BF16)
        for j2 in range(DFT_N2):
            dst = n1 * j2 + grp * DFT_JG
            o_ref[0, dst:dst + DFT_JG, :] = y[j2 * DFT_JG:(j2 + 1) * DFT_JG, :]


def _four_call(u, tables):
    b, s, _ = u.shape
    n1 = s // DFT_N2
    return pl.pallas_call(
        _four_kernel,
        grid=(b,),
        in_specs=[pl.BlockSpec((1, s, FOURIER_WIDTH), lambda bi: (bi, 0, 0))]
                 + [_const_spec(t.shape) for t in tables],
        out_specs=pl.BlockSpec((1, s, FOURIER_WIDTH), lambda bi: (bi, 0, 0)),
        out_shape=jax.ShapeDtypeStruct((b, s, FOURIER_WIDTH), BF16),
        scratch_shapes=[pltpu.VMEM((DFT_N2, 2 * n1, FOURIER_WIDTH), BF16),
                        pltpu.VMEM((2 * s, FOURIER_WIDTH), BF16)],
        compiler_params=_params("parallel"),
        name="four",
    )(u, *tables)


ATTN_MIN_SUM = 2.0 ** -64
ATTN_BOUND_SLACK = 1.0 + 2.0 ** -6


ATTN_HEADS_PER_STEP = 2


def _attn_kernel(q_ref, k_ref, vt_ref, qsq_ref, ksq_ref, o_ref, *, tq):
    nq = q_ref.shape[1] // tq

    def tile(hh, cols, m_of):
        qk = slice(hh * QK_PAD, (hh + 1) * QK_PAD)
        st = _dot_nt(k_ref[0, :, qk], q_ref[0, cols, qk])
        p = jnp.exp2(st - m_of(st))
        ot = _dot(vt_ref[0, hh * V_AUG:(hh + 1) * V_AUG, :], p.astype(BF16))
        l = ot[V_HEAD:V_HEAD + 1]
        o_ref[0, cols, hh * V_HEAD:(hh + 1) * V_HEAD] = (ot[:V_HEAD] * (1.0 / l)).T.astype(BF16)
        return l

    l_min = None
    for hh in range(ATTN_HEADS_PER_STEP):
        k_max = jnp.sqrt(jnp.max(ksq_ref[0, hh], axis=-1, keepdims=True)) * ATTN_BOUND_SLACK
        for j in range(nq):
            cols = slice(j * tq, (j + 1) * tq)
            bound = jnp.sqrt(qsq_ref[0, hh, :, cols]) * k_max
            l = tile(hh, cols, lambda st: bound)
            l_min = l if l_min is None else jnp.minimum(l_min, l)

    @pl.when(jnp.logical_not(jnp.min(l_min) >= ATTN_MIN_SUM))
    def _():
        for hh in range(ATTN_HEADS_PER_STEP):
            def body(j, carry):
                cols = pl.ds(pl.multiple_of(j * tq, tq), tq)
                tile(hh, cols, lambda st: jnp.max(st, axis=0, keepdims=True))
                return carry
            lax.fori_loop(0, nq, body, 0)


def _attn_call(q, k, vt, qsq, ksq, tq):
    b, s, _ = q.shape
    hps = ATTN_HEADS_PER_STEP
    norm_spec = pl.BlockSpec((1, hps, 1, s), lambda bi, h: (bi, h, 0, 0))
    return pl.pallas_call(
        functools.partial(_attn_kernel, tq=tq),
        grid=(b, N_HEADS // hps),
        in_specs=[pl.BlockSpec((1, s, hps * QK_PAD), lambda bi, h: (bi, 0, h)),
                  pl.BlockSpec((1, s, hps * QK_PAD), lambda bi, h: (bi, 0, h)),
                  pl.BlockSpec((1, hps * V_AUG, s), lambda bi, h: (bi, h, 0)),
                  norm_spec, norm_spec],
        out_specs=pl.BlockSpec((1, s, hps * V_HEAD), lambda bi, h: (bi, 0, h)),
        out_shape=jax.ShapeDtypeStruct((b, s, N_HEADS * V_HEAD), BF16),
        compiler_params=_params("parallel", "parallel"),
        name="attn",
    )(q, k, vt, qsq.reshape(b, N_HEADS, 1, s), ksq.reshape(b, N_HEADS, 1, s))


def _merge_kernel(a_ref, f_ref, g_ref, x_ref, gt_ref, gpost_ref, wao_ref, wfo_ref, wout_ref, o_ref):
    half = x_ref.shape[1] // 2
    halves = (slice(0, half), slice(half, 2 * half))
    branches = [(_dot(a_ref[0, rows, :], wao_ref[...]), _dot(f_ref[0, rows, :], wfo_ref[...])) for rows in halves]
    ys = []
    for rows, (branch_a, branch_b) in zip(halves, branches):
        gate_a = g_ref[0, rows, :D_MODEL].astype(F32)
        gate_b = g_ref[0, rows, D_MODEL:].astype(F32)
        merged = jax.nn.sigmoid(gate_a) * branch_a + jax.nn.sigmoid(gate_b) * branch_b
        ys.append(_dot(merged.astype(BF16), wout_ref[...]))
    for rows, y in zip(halves, ys):
        o_ref[0, rows, :] = x_ref[0, rows, :] + gt_ref[0] * _rms(y, gpost_ref[...])


def _merge_call(attn, four, gates, x, gt, lw, tm):
    b, s, _ = x.shape
    row = lambda bi, i: (bi, i, 0)
    weights = [lw["g_mix_post"], lw["w_attn_o"], lw["w_four"], lw["w_out"]]
    return pl.pallas_call(
        _merge_kernel,
        grid=(b, s // tm),
        in_specs=[pl.BlockSpec((1, tm, N_HEADS * V_HEAD), row),
                  pl.BlockSpec((1, tm, FOURIER_WIDTH), row),
                  pl.BlockSpec((1, tm, 2 * D_MODEL), row),
                  pl.BlockSpec((1, tm, D_MODEL), row),
                  pl.BlockSpec((1, 1, D_MODEL), lambda bi, i: (bi, 0, 0))]
                 + [_const_spec(w.shape) for w in weights],
        out_specs=pl.BlockSpec((1, tm, D_MODEL), row),
        out_shape=jax.ShapeDtypeStruct(x.shape, F32),
        compiler_params=_params("parallel", "parallel"),
        name="merge",
    )(attn, four, gates, x, gt, *weights)


FFN_CHUNK = 2 * LANES
assert D_FF % FFN_CHUNK == 0
GELU_C = math.sqrt(2.0 / math.pi)


def _gelu_tanh(a):
    return 0.5 * a * (1.0 + jnp.tanh(a * (GELU_C + (GELU_C * 0.044715) * (a * a))))


def _ffn_kernel(xp_ref, x_ref, xn_ref, sh_ref, sc_ref, gt_ref, gpre_ref, gpost_ref,
                wua_ref, wub_ref, wca_ref, wcb_ref, bca_ref, bcb_ref, wd_ref, o_ref, act_ref):
    i = pl.program_id(1)
    last = pl.num_programs(1) - 1
    tm = x_ref.shape[1]
    te = tm + 2 * SUBLANES
    x = x_ref[0]
    xe = jnp.concatenate([xp_ref[0], x, xn_ref[0]], axis=0)
    he = _rms(xe, gpre_ref[...]) * (1.0 + sc_ref[0]) + sh_ref[0]
    r = lax.broadcasted_iota(jnp.int32, (te, 1), 0)
    inside = jnp.logical_and(jnp.logical_or(r >= SUBLANES, i > 0),
                             jnp.logical_or(r < tm + SUBLANES, i < last))
    he = jnp.where(inside, he, 0.0).astype(BF16)

    def conv(u, w_ref, b_ref, cs):
        prev = pltpu.roll(u, 1, 0)[SUBLANES:SUBLANES + tm]
        nxt = pltpu.roll(u, te - 1, 0)[SUBLANES:SUBLANES + tm]
        mid = u[SUBLANES:SUBLANES + tm]
        return prev * w_ref[0:1, cs] + mid * w_ref[1:2, cs] + nxt * w_ref[2:3, cs] + b_ref[:, cs]

    for c in range(D_FF // FFN_CHUNK):
        cs = slice(c * FFN_CHUNK, (c + 1) * FFN_CHUNK)
        a = conv(_dot(he, wua_ref[:, cs]), wca_ref, bca_ref, cs)
        g = conv(_dot(he, wub_ref[:, cs]), wcb_ref, bcb_ref, cs)
        act_ref[:, cs] = (_gelu_tanh(a) * g).astype(BF16)

    y = _dot(act_ref[...], wd_ref[...])
    o_ref[0] = x + gt_ref[0] * _rms(y, gpost_ref[...])


def _ffn_call(x, sh, sc, gt, lw, tm):
    b, s, _ = x.shape
    nb = tm // SUBLANES
    row = lambda bi, i: (bi, i, 0)
    vec = lambda bi, i: (bi, 0, 0)
    weights = [lw["g_ffn_pre"], lw["g_ffn_post"], lw["w_up_a"], lw["w_up_b"],
               lw["w_conv_a"], lw["w_conv_b"], lw["b_conv_a"], lw["b_conv_b"], lw["w_down"]]
    return pl.pallas_call(
        _ffn_kernel,
        grid=(b, s // tm),
        in_specs=[pl.BlockSpec((1, SUBLANES, D_MODEL), lambda bi, i: (bi, jnp.maximum(i * nb - 1, 0), 0)),
                  pl.BlockSpec((1, tm, D_MODEL), row),
                  pl.BlockSpec((1, SUBLANES, D_MODEL),
                               lambda bi, i: (bi, jnp.minimum((i + 1) * nb, s // SUBLANES - 1), 0)),
                  pl.BlockSpec((1, 1, D_MODEL), vec),
                  pl.BlockSpec((1, 1, D_MODEL), vec),
                  pl.BlockSpec((1, 1, D_MODEL), vec)]
                 + [_const_spec(w.shape) for w in weights],
        out_specs=pl.BlockSpec((1, tm, D_MODEL), row),
        out_shape=jax.ShapeDtypeStruct(x.shape, F32),
        scratch_shapes=[pltpu.VMEM((tm, D_FF), BF16)],
        compiler_params=_params("parallel", "parallel"),
        name="ffn",
    )(x, x, x, sh, sc, gt, *weights)


def _rope_tables(s):
    half = QK_ROPE // 2
    inv = 1.0 / (ROPE_BASE ** (jnp.arange(half, dtype=F32) / half))
    ang = jnp.arange(s, dtype=F32)[:, None] * inv[None, :]
    cos, sin = jnp.cos(ang), jnp.sin(ang)
    zero = jnp.zeros((s, LANES - QK_ROPE), F32)
    return (jnp.concatenate([cos, cos, zero], axis=1),
            jnp.concatenate([-sin, sin, zero], axis=1))


def _dft_cos_sin(n):
    idx = jnp.arange(n, dtype=jnp.int32)
    ang = ((idx[:, None] * idx[None, :]) % n).astype(F32) * (2.0 * math.pi / n)
    norm = n ** -0.5
    return jnp.cos(ang) * norm, jnp.sin(ang) * norm


def _fourier_tables(s):
    n1 = s // DFT_N2
    rows = BF16_SUBLANES
    new = jnp.arange(DFT_PB, dtype=jnp.int32)
    old = DFT_N2 * (new % rows) + new // rows
    perm = (old[:, None] == jnp.arange(DFT_PB, dtype=jnp.int32)[None, :]).astype(BF16)
    cc, sc_ = _dft_cos_sin(FOURIER_GROUP)
    wcs = jnp.concatenate([cc, sc_], axis=1).astype(BF16)
    j1 = jnp.arange(n1, dtype=jnp.int32)[None, :, None]
    s1 = jnp.arange(n1, dtype=jnp.int32)[None, None, :]
    s2 = jnp.arange(DFT_N2, dtype=jnp.int32)[:, None, None]
    ang = ((j1 * (DFT_N2 * s1 + s2)) % s).astype(F32) * (2.0 * math.pi / s)
    c1, sn1 = jnp.cos(ang) * s ** -0.5, jnp.sin(ang) * s ** -0.5
    w1 = jnp.concatenate([jnp.concatenate([c1, -sn1], axis=2),
                          jnp.concatenate([sn1, c1], axis=2)], axis=1).astype(BF16)
    j2 = jnp.arange(DFT_N2, dtype=jnp.int32)
    ang2 = ((j2[:, None] * j2[None, :]) % DFT_N2).astype(F32) * (2.0 * math.pi / DFT_N2)
    eye = jnp.eye(DFT_JG, dtype=F32)
    bd = jnp.concatenate([jnp.kron(jnp.cos(ang2), eye), jnp.kron(-jnp.sin(ang2), eye)], axis=1).astype(BF16)
    return perm, wcs, w1, bd


def _layer_weights(l, g_mix_pre, g_mix_post, w_in, g_q, w_q_b, g_kv, w_kv_b, w_attn_o, w_four, w_out,
                   g_ffn_pre, g_ffn_post, w_up, w_conv, b_conv, w_down):
    wi = w_in[l]
    wqb = w_q_b[l].reshape(Q_LORA, N_HEADS, QK_HEAD)
    wkvb = w_kv_b[l].reshape(KV_LORA, N_HEADS, QK_NOPE + V_HEAD)
    rope_pad = ((0, 0), (0, 0), (0, LANES - QK_ROPE))
    return {
        "g_mix_pre": g_mix_pre[l].reshape(1, D_MODEL),
        "g_mix_post": g_mix_post[l].reshape(1, D_MODEL),
        "w_qa": wi[:, :OFF_KV].astype(BF16),
        "w_kva": wi[:, OFF_KV:OFF_KR].astype(BF16),
        "w_kr": jnp.pad(wi[:, OFF_KR:OFF_F], ((0, 0), (0, LANES - QK_ROPE))).astype(BF16),
        "w_f": wi[:, OFF_F:OFF_G].astype(BF16),
        "w_g": wi[:, OFF_G:].astype(BF16),
        "g_q": g_q[l].reshape(1, Q_LORA),
        "g_kv": g_kv[l].reshape(1, KV_LORA),
        "w_qn": wqb[:, :, :QK_NOPE].reshape(Q_LORA, N_HEADS * QK_NOPE).astype(BF16),
        "w_qr": jnp.pad(wqb[:, :, QK_NOPE:], rope_pad).reshape(Q_LORA, N_HEADS * LANES).astype(BF16),
        "w_kb": wkvb[:, :, :QK_NOPE].reshape(KV_LORA, N_HEADS * QK_NOPE).astype(BF16),
        "w_vbt": wkvb[:, :, QK_NOPE:].reshape(KV_LORA, N_HEADS * V_HEAD).T.astype(BF16),
        "w_attn_o": w_attn_o[l].astype(BF16),
        "w_four": w_four[l].astype(BF16),
        "w_out": w_out[l].astype(BF16),
        "g_ffn_pre": g_ffn_pre[l].reshape(1, D_MODEL),
        "g_ffn_post": g_ffn_post[l].reshape(1, D_MODEL),
        "w_up_a": w_up[l][:, :D_FF].astype(BF16),
        "w_up_b": w_up[l][:, D_FF:].astype(BF16),
        "w_conv_a": w_conv[l][:, :D_FF],
        "w_conv_b": w_conv[l][:, D_FF:],
        "b_conv_a": b_conv[l][:D_FF].reshape(1, D_FF),
        "b_conv_b": b_conv[l][D_FF:].reshape(1, D_FF),
        "w_down": w_down[l].astype(BF16),
    }


def _tile(s, want):
    t = min(s, want)
    assert s % t == 0
    return t


def _trunk(x, c, w_ada, b_ada, layers):
    b, s, _ = x.shape
    tm = _tile(s, 512)
    tm_big = _tile(s, 1024)
    tq = _tile(s, 512)
    assert s % DFT_PB == 0 and (s // DFT_N2) % DFT_JG == 0
    cos_t, sin_t = _rope_tables(s)
    four_tables = _fourier_tables(s)
    mods = _ada_mods(c, w_ada, b_ada).reshape(DEPTH, b, N_ADA, 1, D_MODEL)
    for l, lw in enumerate(layers):
        sh1, sc1, gt1, sh2, sc2, gt2 = [mods[l, :, j] for j in range(N_ADA)]
        q, k, vt, u, gates, qsq, ksq = _pre_call(x, sh1, sc1, lw, cos_t, sin_t, tm)
        four = _four_call(u, four_tables)
        attn = _attn_call(q, k, vt, qsq, ksq, tq)
        x = _merge_call(attn, four, gates, x, gt1, lw, tm_big)
        x = _ffn_call(x, sh2, sc2, gt2, lw, tm_big)
    return x


def kernel(x_prompt, x_sample, c_prompt, c_sample, w_ada, b_ada, g_mix_pre, g_mix_post, w_in, g_q, w_q_b,
           g_kv, w_kv_b, w_attn_o, w_four, w_out, g_ffn_pre, g_ffn_post, w_up, w_conv, b_conv, w_down):
    layers = [_layer_weights(l, g_mix_pre, g_mix_post, w_in, g_q, w_q_b, g_kv, w_kv_b, w_attn_o, w_four,
                             w_out, g_ffn_pre, g_ffn_post, w_up, w_conv, b_conv, w_down)
              for l in range(DEPTH)]
    y_prompt = _trunk(x_prompt, c_prompt, w_ada, b_ada, layers)
    y_sample = _trunk(x_sample, c_sample, w_ada, b_ada, layers)
    return (y_prompt, y_sample)
```

```python
import functools
import math

import jax
import jax.numpy as jnp
from jax import lax
from jax.experimental import pallas as pl
from jax.experimental.pallas import tpu as pltpu

D_MODEL = 1024
DEPTH = 4
N_HEADS = 8
QK_NOPE = 128
QK_ROPE = 64
V_HEAD = 128
Q_LORA = 512
KV_LORA = 256
QK_HEAD = QK_NOPE + QK_ROPE
ROPE_BASE = 10000.0
N_FOURIER_GROUPS = 4
FOURIER_GROUP = 128
FOURIER_WIDTH = N_FOURIER_GROUPS * FOURIER_GROUP
D_FF = 2816
EPS = 1e-6
N_ADA = 6
OFF_KV = Q_LORA
OFF_KR = OFF_KV + KV_LORA
OFF_F = OFF_KR + QK_ROPE
OFF_G = OFF_F + FOURIER_WIDTH

LANES = 128
SUBLANES = 8
QK_PAD = 2 * LANES
BF16_SUBLANES = 16
V_AUG = V_HEAD + BF16_SUBLANES
VMEM_LIMIT_BYTES = 60 * 1024 * 1024

BF16 = jnp.bfloat16
F32 = jnp.float32


def _dot(a, b):
    return jnp.dot(a, b, preferred_element_type=F32)


def _dot_nt(a, b):
    return lax.dot_general(a, b, (((1,), (1,)), ((), ())), preferred_element_type=F32)


def _rms(x, g):
    return x * lax.rsqrt(jnp.mean(x * x, axis=-1, keepdims=True) + EPS) * g


def _const_spec(shape):
    nd = len(shape)
    return pl.BlockSpec(shape, lambda *_: (0,) * nd, pipeline_mode=pl.Buffered(1))


def _params(*sem):
    return pltpu.CompilerParams(dimension_semantics=sem, vmem_limit_bytes=VMEM_LIMIT_BYTES)


def _ada_kernel(c_ref, w_ref, b_ref, o_ref):
    c = c_ref[...]
    s = c * jax.nn.sigmoid(c)
    o_ref[0] = jnp.dot(s, w_ref[0], preferred_element_type=F32,
                       precision=lax.Precision.HIGHEST) + b_ref[0]


def _ada_mods(c, w_ada, b_ada):
    b = c.shape[0]
    bp = -(-b // SUBLANES) * SUBLANES
    cp = jnp.pad(c, ((0, bp - b), (0, 0)))
    out = pl.pallas_call(
        _ada_kernel,
        grid=(DEPTH, N_ADA),
        in_specs=[
            pl.BlockSpec((bp, D_MODEL), lambda l, j: (0, 0)),
            pl.BlockSpec((1, D_MODEL, D_MODEL), lambda l, j: (l, 0, j)),
            pl.BlockSpec((1, 1, D_MODEL), lambda l, j: (l, 0, j)),
        ],
        out_specs=pl.BlockSpec((1, bp, D_MODEL), lambda l, j: (l, 0, j)),
        out_shape=jax.ShapeDtypeStruct((DEPTH, bp, N_ADA * D_MODEL), F32),
        compiler_params=_params("arbitrary", "arbitrary"),
        name="ada",
    )(cp, w_ada, b_ada.reshape(DEPTH, 1, N_ADA * D_MODEL))
    return out[:, :b]


def _pre_kernel(x_ref, sh_ref, sc_ref, gpre_ref, wqa_ref, wkva_ref, wkr_ref, wf_ref, wg_ref,
                gq_ref, gkv_ref, wqn_ref, wqr_ref, wkb_ref, wvbt_ref, hsum_ref, cos_ref, sin_ref,
                q_ref, k_ref, vt_ref, u_ref, g_ref, qsq_ref, ksq_ref):
    tm = x_ref.shape[1]
    x = x_ref[0]
    h = _rms(x, gpre_ref[...]) * (1.0 + sc_ref[0]) + sh_ref[0]
    hb = h.astype(BF16)

    zq = _dot(hb, wqa_ref[...])
    zkv = _dot(hb, wkva_ref[...])
    zkr = _dot(hb, wkr_ref[...])
    cq = _rms(zq, gq_ref[...]).astype(BF16)
    ckv = _rms(zkv, gkv_ref[...]).astype(BF16)

    cos = cos_ref[...]
    sin = sin_ref[...]
    lane = lax.broadcasted_iota(jnp.int32, (tm, LANES), 1)
    low_half = (lane % QK_ROPE) < (QK_ROPE // 2)

    def rope(t):
        rot = jnp.where(low_half, pltpu.roll(t, LANES - QK_ROPE // 2, 1), pltpu.roll(t, QK_ROPE // 2, 1))
        return t * cos + rot * sin

    scale = QK_HEAD ** -0.5 * math.log2(math.e)
    kr = rope(zkr).astype(BF16)
    qn = _dot(cq, wqn_ref[...])
    qr = _dot(cq, wqr_ref[...])
    kn = _dot(ckv, wkb_ref[...])
    u_ref[0] = _dot(hb, wf_ref[...]).astype(BF16)
    g_ref[0] = _dot(hb, wg_ref[...]).astype(BF16)
    vt = _dot_nt(wvbt_ref[...], ckv)
    extra = lax.broadcasted_iota(jnp.int32, (V_AUG - V_HEAD, tm), 0)
    ones_row = jnp.where(extra == 0, 1.0, 0.0).astype(BF16)
    for hd in range(N_HEADS):
        vt_ref[0, hd * V_AUG:hd * V_AUG + V_HEAD, :] = vt[hd * V_HEAD:(hd + 1) * V_HEAD].astype(BF16)
        vt_ref[0, hd * V_AUG + V_HEAD:(hd + 1) * V_AUG, :] = ones_row

    def sq(t):
        t = t.astype(F32)
        return t * t

    kr_sq = sq(kr)
    q_sq, k_sq = [], []
    for hd in range(N_HEADS):
        lo = hd * QK_PAD
        sl = slice(hd * LANES, (hd + 1) * LANES)
        q_nope = (qn[:, sl] * scale).astype(BF16)
        q_rope = (rope(qr[:, sl]) * scale).astype(BF16)
        k_nope = kn[:, sl].astype(BF16)
        q_ref[0, :, lo:lo + LANES] = q_nope
        q_ref[0, :, lo + LANES:lo + QK_PAD] = q_rope
        k_ref[0, :, lo:lo + LANES] = k_nope
        k_ref[0, :, lo + LANES:lo + QK_PAD] = kr
        q_sq.append((sq(q_nope) + sq(q_rope)).astype(BF16))
        k_sq.append((sq(k_nope) + kr_sq).astype(BF16))
    qsq_ref[0] = _dot_nt(hsum_ref[...], jnp.concatenate(q_sq, axis=1))[:N_HEADS]
    ksq_ref[0] = _dot_nt(hsum_ref[...], jnp.concatenate(k_sq, axis=1))[:N_HEADS]


def _head_sum_matrix():
    r = jnp.arange(BF16_SUBLANES, dtype=jnp.int32)[:, None]
    c = jnp.arange(N_HEADS * LANES, dtype=jnp.int32)[None, :] // LANES
    return (r == c).astype(BF16)


def _pre_call(x, sh, sc, lw, cos_t, sin_t, tm):
    b, s, _ = x.shape
    row = lambda bi, i: (bi, i, 0)
    vec = lambda bi, i: (bi, 0, 0)
    weights = [lw["g_mix_pre"], lw["w_qa"], lw["w_kva"], lw["w_kr"], lw["w_f"], lw["w_g"],
               lw["g_q"], lw["g_kv"], lw["w_qn"], lw["w_qr"], lw["w_kb"], lw["w_vbt"], _head_sum_matrix()]
    return pl.pallas_call(
        _pre_kernel,
        grid=(b, s // tm),
        in_specs=[pl.BlockSpec((1, tm, D_MODEL), row),
                  pl.BlockSpec((1, 1, D_MODEL), vec),
                  pl.BlockSpec((1, 1, D_MODEL), vec)]
                 + [_const_spec(w.shape) for w in weights]
                 + [pl.BlockSpec((tm, LANES), lambda bi, i: (i, 0)),
                    pl.BlockSpec((tm, LANES), lambda bi, i: (i, 0))],
        out_specs=[pl.BlockSpec((1, tm, N_HEADS * QK_PAD), row),
                   pl.BlockSpec((1, tm, N_HEADS * QK_PAD), row),
                   pl.BlockSpec((1, N_HEADS * V_AUG, tm), lambda bi, i: (bi, 0, i)),
                   pl.BlockSpec((1, tm, FOURIER_WIDTH), row),
                   pl.BlockSpec((1, tm, 2 * D_MODEL), row),
                   pl.BlockSpec((1, N_HEADS, tm), lambda bi, i: (bi, 0, i)),
                   pl.BlockSpec((1, N_HEADS, tm), lambda bi, i: (bi, 0, i))],
        out_shape=[jax.ShapeDtypeStruct((b, s, N_HEADS * QK_PAD), BF16),
                   jax.ShapeDtypeStruct((b, s, N_HEADS * QK_PAD), BF16),
                   jax.ShapeDtypeStruct((b, N_HEADS * V_AUG, s), BF16),
                   jax.ShapeDtypeStruct((b, s, FOURIER_WIDTH), BF16),
                   jax.ShapeDtypeStruct((b, s, 2 * D_MODEL), BF16),
                   jax.ShapeDtypeStruct((b, N_HEADS, s), F32),
                   jax.ShapeDtypeStruct((b, N_HEADS, s), F32)],
        compiler_params=_params("parallel", "parallel"),
        name="pre",
    )(x, sh, sc, *weights, cos_t, sin_t)


DFT_N2 = 16
DFT_JG = BF16_SUBLANES
DFT_PB = DFT_N2 * BF16_SUBLANES


def _four_kernel(u_ref, perm_ref, wcs_ref, w1_ref, bd_ref, o_ref, ab_ref, z_ref):
    s = u_ref.shape[1]
    n1 = s // DFT_N2
    rows = BF16_SUBLANES
    for blk in range(s // DFT_PB):
        ub = u_ref[0, blk * DFT_PB:(blk + 1) * DFT_PB, :]
        up = _dot(perm_ref[...], ub).astype(BF16)
        for g in range(N_FOURIER_GROUPS):
            sl = slice(g * FOURIER_GROUP, (g + 1) * FOURIER_GROUP)
            ab = _dot(up[:, sl], wcs_ref[...]).astype(BF16)
            for s2 in range(DFT_N2):
                src = slice(s2 * rows, (s2 + 1) * rows)
                dst = slice(blk * rows, (blk + 1) * rows)
                ab_ref[s2, dst, sl] = ab[src, :FOURIER_GROUP]
                ab_ref[s2, n1 + blk * rows:n1 + (blk + 1) * rows, sl] = ab[src, FOURIER_GROUP:]
    kg = 2 * DFT_N2 * DFT_JG
    for s2 in range(DFT_N2):
        z = _dot(w1_ref[s2], ab_ref[s2]).astype(BF16)
        for grp in range(n1 // DFT_JG):
            for r in range(2):
                dst = grp * kg + r * DFT_N2 * DFT_JG + s2 * DFT_JG
                z_ref[dst:dst + DFT_JG, :] = z[r * n1 + grp * DFT_JG:r * n1 + (grp + 1) * DFT_JG, :]
    for grp in range(n1 // DFT_JG):
        y = _dot(bd_ref[...], z_ref[grp * kg:(grp + 1) * kg, :]).astype(BF16)
        for j2 in range(DFT_N2):
            dst = n1 * j2 + grp * DFT_JG
            o_ref[0, dst:dst + DFT_JG, :] = y[j2 * DFT_JG:(j2 + 1) * DFT_JG, :]


def _four_call(u, tables):
    b, s, _ = u.shape
    n1 = s // DFT_N2
    return pl.pallas_call(
        _four_kernel,
        grid=(b,),
        in_specs=[pl.BlockSpec((1, s, FOURIER_WIDTH), lambda bi: (bi, 0, 0))]
                 + [_const_spec(t.shape) for t in tables],
        out_specs=pl.BlockSpec((1, s, FOURIER_WIDTH), lambda bi: (bi, 0, 0)),
        out_shape=jax.ShapeDtypeStruct((b, s, FOURIER_WIDTH), BF16),
        scratch_shapes=[pltpu.VMEM((DFT_N2, 2 * n1, FOURIER_WIDTH), BF16),
                        pltpu.VMEM((2 * s, FOURIER_WIDTH), BF16)],
        compiler_params=_params("parallel"),
        name="four",
    )(u, *tables)


ATTN_MIN_SUM = 2.0 ** -64
ATTN_BOUND_SLACK = 1.0 + 2.0 ** -6


ATTN_STEP_KEY_ROWS = 4096


def _attn_kernel(q_ref, k_ref, vt_ref, qsq_ref, ksq_ref, o_ref, *, tq, heads):
    nq = q_ref.shape[1] // tq

    def tile(hh, cols, m_of):
        qk = slice(hh * QK_PAD, (hh + 1) * QK_PAD)
        st = _dot_nt(k_ref[0, :, qk], q_ref[0, cols, qk])
        p = jnp.exp2(st - m_of(st))
        ot = _dot(vt_ref[0, hh * V_AUG:(hh + 1) * V_AUG, :], p.astype(BF16))
        l = ot[V_HEAD:V_HEAD + 1]
        o_ref[0, cols, hh * V_HEAD:(hh + 1) * V_HEAD] = (ot[:V_HEAD] * (1.0 / l)).T.astype(BF16)
        return l

    l_min = None
    for hh in range(heads):
        k_max = jnp.sqrt(jnp.max(ksq_ref[0, hh], axis=-1, keepdims=True)) * ATTN_BOUND_SLACK
        for j in range(nq):
            cols = slice(j * tq, (j + 1) * tq)
            bound = jnp.sqrt(qsq_ref[0, hh, :, cols]) * k_max
            l = tile(hh, cols, lambda st: bound)
            l_min = l if l_min is None else jnp.minimum(l_min, l)

    @pl.when(jnp.logical_not(jnp.min(l_min) >= ATTN_MIN_SUM))
    def _():
        for hh in range(heads):
            def body(j, carry):
                cols = pl.ds(pl.multiple_of(j * tq, tq), tq)
                tile(hh, cols, lambda st: jnp.max(st, axis=0, keepdims=True))
                return carry
            lax.fori_loop(0, nq, body, 0)


def _attn_call(q, k, vt, qsq, ksq, tq):
    b, s, _ = q.shape
    hps = max(1, min(2, ATTN_STEP_KEY_ROWS // s))
    norm_spec = pl.BlockSpec((1, hps, 1, s), lambda bi, h: (bi, h, 0, 0))
    return pl.pallas_call(
        functools.partial(_attn_kernel, tq=tq, heads=hps),
        grid=(b, N_HEADS // hps),
        in_specs=[pl.BlockSpec((1, s, hps * QK_PAD), lambda bi, h: (bi, 0, h)),
                  pl.BlockSpec((1, s, hps * QK_PAD), lambda bi, h: (bi, 0, h)),
                  pl.BlockSpec((1, hps * V_AUG, s), lambda bi, h: (bi, h, 0)),
                  norm_spec, norm_spec],
        out_specs=pl.BlockSpec((1, s, hps * V_HEAD), lambda bi, h: (bi, 0, h)),
        out_shape=jax.ShapeDtypeStruct((b, s, N_HEADS * V_HEAD), BF16),
        compiler_params=_params("parallel", "parallel"),
        name="attn",
    )(q, k, vt, qsq.reshape(b, N_HEADS, 1, s), ksq.reshape(b, N_HEADS, 1, s))


def _merge_kernel(a_ref, f_ref, g_ref, x_ref, gt_ref, gpost_ref, wao_ref, wfo_ref, wout_ref, o_ref):
    half = x_ref.shape[1] // 2
    halves = (slice(0, half), slice(half, 2 * half))
    branches = [(_dot(a_ref[0, rows, :], wao_ref[...]), _dot(f_ref[0, rows, :], wfo_ref[...])) for rows in halves]
    ys = []
    for rows, (branch_a, branch_b) in zip(halves, branches):
        gate_a = g_ref[0, rows, :D_MODEL].astype(F32)
        gate_b = g_ref[0, rows, D_MODEL:].astype(F32)
        merged = jax.nn.sigmoid(gate_a) * branch_a + jax.nn.sigmoid(gate_b) * branch_b
        ys.append(_dot(merged.astype(BF16), wout_ref[...]))
    for rows, y in zip(halves, ys):
        o_ref[0, rows, :] = x_ref[0, rows, :] + gt_ref[0] * _rms(y, gpost_ref[...])


def _merge_call(attn, four, gates, x, gt, lw, tm):
    b, s, _ = x.shape
    row = lambda bi, i: (bi, i, 0)
    weights = [lw["g_mix_post"], lw["w_attn_o"], lw["w_four"], lw["w_out"]]
    return pl.pallas_call(
        _merge_kernel,
        grid=(b, s // tm),
        in_specs=[pl.BlockSpec((1, tm, N_HEADS * V_HEAD), row),
                  pl.BlockSpec((1, tm, FOURIER_WIDTH), row),
                  pl.BlockSpec((1, tm, 2 * D_MODEL), row),
                  pl.BlockSpec((1, tm, D_MODEL), row),
                  pl.BlockSpec((1, 1, D_MODEL), lambda bi, i: (bi, 0, 0))]
                 + [_const_spec(w.shape) for w in weights],
        out_specs=pl.BlockSpec((1, tm, D_MODEL), row),
        out_shape=jax.ShapeDtypeStruct(x.shape, F32),
        compiler_params=_params("parallel", "parallel"),
        name="merge",
    )(attn, four, gates, x, gt, *weights)


FFN_CHUNK = 2 * LANES
assert D_FF % FFN_CHUNK == 0
GELU_C = math.sqrt(2.0 / math.pi)


def _gelu_tanh_x2(a):
    return a * (1.0 + jnp.tanh(a * (GELU_C + (GELU_C * 0.044715) * (a * a))))


def _ffn_kernel(xp_ref, x_ref, xn_ref, sh_ref, sc_ref, gt_ref, gpre_ref, gpost_ref,
                wua_ref, wub_ref, wca_ref, wcb_ref, bca_ref, bcb_ref, wd_ref, o_ref, act_ref):
    i = pl.program_id(1)
    last = pl.num_programs(1) - 1
    tm = x_ref.shape[1]
    te = tm + 2 * SUBLANES
    x = x_ref[0]
    xe = jnp.concatenate([xp_ref[0], x, xn_ref[0]], axis=0)
    he = _rms(xe, gpre_ref[...]) * (1.0 + sc_ref[0]) + sh_ref[0]
    r = lax.broadcasted_iota(jnp.int32, (te, 1), 0)
    inside = jnp.logical_and(jnp.logical_or(r >= SUBLANES, i > 0),
                             jnp.logical_or(r < tm + SUBLANES, i < last))
    he = jnp.where(inside, he, 0.0).astype(BF16)

    def conv(u, w_ref, b_ref, cs):
        prev = pltpu.roll(u, 1, 0)[SUBLANES:SUBLANES + tm]
        nxt = pltpu.roll(u, te - 1, 0)[SUBLANES:SUBLANES + tm]
        mid = u[SUBLANES:SUBLANES + tm]
        return prev * w_ref[0:1, cs] + mid * w_ref[1:2, cs] + nxt * w_ref[2:3, cs] + b_ref[:, cs]

    for c in range(D_FF // FFN_CHUNK):
        cs = slice(c * FFN_CHUNK, (c + 1) * FFN_CHUNK)
        a = conv(_dot(he, wua_ref[:, cs]), wca_ref, bca_ref, cs)
        g = conv(_dot(he, wub_ref[:, cs]), wcb_ref, bcb_ref, cs)
        act_ref[:, cs] = (_gelu_tanh_x2(a) * g).astype(BF16)

    y = _dot(act_ref[...], wd_ref[...])
    o_ref[0] = x + gt_ref[0] * _rms(y, gpost_ref[...])


def _ffn_call(x, sh, sc, gt, lw, tm):
    b, s, _ = x.shape
    nb = tm // SUBLANES
    row = lambda bi, i: (bi, i, 0)
    vec = lambda bi, i: (bi, 0, 0)
    weights = [lw["g_ffn_pre"], lw["g_ffn_post"], lw["w_up_a"], lw["w_up_b"],
               lw["w_conv_a"], lw["w_conv_b"], lw["b_conv_a"], lw["b_conv_b"], lw["w_down"]]
    return pl.pallas_call(
        _ffn_kernel,
        grid=(b, s // tm),
        in_specs=[pl.BlockSpec((1, SUBLANES, D_MODEL), lambda bi, i: (bi, jnp.maximum(i * nb - 1, 0), 0)),
                  pl.BlockSpec((1, tm, D_MODEL), row),
                  pl.BlockSpec((1, SUBLANES, D_MODEL),
                               lambda bi, i: (bi, jnp.minimum((i + 1) * nb, s // SUBLANES - 1), 0)),
                  pl.BlockSpec((1, 1, D_MODEL), vec),
                  pl.BlockSpec((1, 1, D_MODEL), vec),
                  pl.BlockSpec((1, 1, D_MODEL), vec)]
                 + [_const_spec(w.shape) for w in weights],
        out_specs=pl.BlockSpec((1, tm, D_MODEL), row),
        out_shape=jax.ShapeDtypeStruct(x.shape, F32),
        scratch_shapes=[pltpu.VMEM((tm, D_FF), BF16)],
        compiler_params=_params("parallel", "parallel"),
        name="ffn",
    )(x, x, x, sh, sc, gt, *weights)


def _rope_tables(s):
    half = QK_ROPE // 2
    inv = 1.0 / (ROPE_BASE ** (jnp.arange(half, dtype=F32) / half))
    ang = jnp.arange(s, dtype=F32)[:, None] * inv[None, :]
    cos, sin = jnp.cos(ang), jnp.sin(ang)
    zero = jnp.zeros((s, LANES - QK_ROPE), F32)
    return (jnp.concatenate([cos, cos, zero], axis=1),
            jnp.concatenate([-sin, sin, zero], axis=1))


def _dft_cos_sin(n):
    idx = jnp.arange(n, dtype=jnp.int32)
    ang = ((idx[:, None] * idx[None, :]) % n).astype(F32) * (2.0 * math.pi / n)
    norm = n ** -0.5
    return jnp.cos(ang) * norm, jnp.sin(ang) * norm


def _fourier_tables(s):
    n1 = s // DFT_N2
    rows = BF16_SUBLANES
    new = jnp.arange(DFT_PB, dtype=jnp.int32)
    old = DFT_N2 * (new % rows) + new // rows
    perm = (old[:, None] == jnp.arange(DFT_PB, dtype=jnp.int32)[None, :]).astype(BF16)
    cc, sc_ = _dft_cos_sin(FOURIER_GROUP)
    wcs = jnp.concatenate([cc, sc_], axis=1).astype(BF16)
    j1 = jnp.arange(n1, dtype=jnp.int32)[None, :, None]
    s1 = jnp.arange(n1, dtype=jnp.int32)[None, None, :]
    s2 = jnp.arange(DFT_N2, dtype=jnp.int32)[:, None, None]
    ang = ((j1 * (DFT_N2 * s1 + s2)) % s).astype(F32) * (2.0 * math.pi / s)
    c1, sn1 = jnp.cos(ang) * s ** -0.5, jnp.sin(ang) * s ** -0.5
    w1 = jnp.concatenate([jnp.concatenate([c1, -sn1], axis=2),
                          jnp.concatenate([sn1, c1], axis=2)], axis=1).astype(BF16)
    j2 = jnp.arange(DFT_N2, dtype=jnp.int32)
    ang2 = ((j2[:, None] * j2[None, :]) % DFT_N2).astype(F32) * (2.0 * math.pi / DFT_N2)
    eye = jnp.eye(DFT_JG, dtype=F32)
    bd = jnp.concatenate([jnp.kron(jnp.cos(ang2), eye), jnp.kron(-jnp.sin(ang2), eye)], axis=1).astype(BF16)
    return perm, wcs, w1, bd


def _layer_weights(l, g_mix_pre, g_mix_post, w_in, g_q, w_q_b, g_kv, w_kv_b, w_attn_o, w_four, w_out,
                   g_ffn_pre, g_ffn_post, w_up, w_conv, b_conv, w_down):
    wi = w_in[l]
    wqb = w_q_b[l].reshape(Q_LORA, N_HEADS, QK_HEAD)
    wkvb = w_kv_b[l].reshape(KV_LORA, N_HEADS, QK_NOPE + V_HEAD)
    rope_pad = ((0, 0), (0, 0), (0, LANES - QK_ROPE))
    return {
        "g_mix_pre": g_mix_pre[l].reshape(1, D_MODEL),
        "g_mix_post": g_mix_post[l].reshape(1, D_MODEL),
        "w_qa": wi[:, :OFF_KV].astype(BF16),
        "w_kva": wi[:, OFF_KV:OFF_KR].astype(BF16),
        "w_kr": jnp.pad(wi[:, OFF_KR:OFF_F], ((0, 0), (0, LANES - QK_ROPE))).astype(BF16),
        "w_f": wi[:, OFF_F:OFF_G].astype(BF16),
        "w_g": wi[:, OFF_G:].astype(BF16),
        "g_q": g_q[l].reshape(1, Q_LORA),
        "g_kv": g_kv[l].reshape(1, KV_LORA),
        "w_qn": wqb[:, :, :QK_NOPE].reshape(Q_LORA, N_HEADS * QK_NOPE).astype(BF16),
        "w_qr": jnp.pad(wqb[:, :, QK_NOPE:], rope_pad).reshape(Q_LORA, N_HEADS * LANES).astype(BF16),
        "w_kb": wkvb[:, :, :QK_NOPE].reshape(KV_LORA, N_HEADS * QK_NOPE).astype(BF16),
        "w_vbt": wkvb[:, :, QK_NOPE:].reshape(KV_LORA, N_HEADS * V_HEAD).T.astype(BF16),
        "w_attn_o": w_attn_o[l].astype(BF16),
        "w_four": w_four[l].astype(BF16),
        "w_out": w_out[l].astype(BF16),
        "g_ffn_pre": g_ffn_pre[l].reshape(1, D_MODEL),
        "g_ffn_post": g_ffn_post[l].reshape(1, D_MODEL),
        "w_up_a": w_up[l][:, :D_FF].astype(BF16),
        "w_up_b": w_up[l][:, D_FF:].astype(BF16),
        "w_conv_a": w_conv[l][:, :D_FF],
        "w_conv_b": 0.5 * w_conv[l][:, D_FF:],
        "b_conv_a": b_conv[l][:D_FF].reshape(1, D_FF),
        "b_conv_b": 0.5 * b_conv[l][D_FF:].reshape(1, D_FF),
        "w_down": w_down[l].astype(BF16),
    }


def _tile(s, want):
    t = min(s, want)
    assert s % t == 0
    return t


def _trunk(x, c, w_ada, b_ada, layers):
    b, s, _ = x.shape
    tm = _tile(s, 512)
    tm_big = _tile(s, 1024)
    tq = _tile(s, 512)
    assert s % DFT_PB == 0 and (s // DFT_N2) % DFT_JG == 0
    cos_t, sin_t = _rope_tables(s)
    four_tables = _fourier_tables(s)
    mods = _ada_mods(c, w_ada, b_ada).reshape(DEPTH, b, N_ADA, 1, D_MODEL)
    for l, lw in enumerate(layers):
        sh1, sc1, gt1, sh2, sc2, gt2 = [mods[l, :, j] for j in range(N_ADA)]
        q, k, vt, u, gates, qsq, ksq = _pre_call(x, sh1, sc1, lw, cos_t, sin_t, tm)
        four = _four_call(u, four_tables)
        attn = _attn_call(q, k, vt, qsq, ksq, tq)
        x = _merge_call(attn, four, gates, x, gt1, lw, tm_big)
        x = _ffn_call(x, sh2, sc2, gt2, lw, tm_big)
    return x


def kernel(x_prompt, x_sample, c_prompt, c_sample, w_ada, b_ada, g_mix_pre, g_mix_post, w_in, g_q, w_q_b,
           g_kv, w_kv_b, w_attn_o, w_four, w_out, g_ffn_pre, g_ffn_post, w_up, w_conv, b_conv, w_down):
    layers = [_layer_weights(l, g_mix_pre, g_mix_post, w_in, g_q, w_q_b, g_kv, w_kv_b, w_attn_o, w_four,
                             w_out, g_ffn_pre, g_ffn_post, w_up, w_conv, b_conv, w_down)
              for l in range(DEPTH)]
    y_prompt = _trunk(x_prompt, c_prompt, w_ada, b_ada, layers)
    y_sample = _trunk(x_sample, c_sample, w_ada, b_ada, layers)
    return (y_prompt, y_sample)
```

```python
import functools
import math

import jax
import jax.numpy as jnp
from jax import lax
from jax.experimental import pallas as pl
from jax.experimental.pallas import tpu as pltpu

D_MODEL = 1024
DEPTH = 4
N_HEADS = 8
QK_NOPE = 128
QK_ROPE = 64
V_HEAD = 128
Q_LORA = 512
KV_LORA = 256
QK_HEAD = QK_NOPE + QK_ROPE
ROPE_BASE = 10000.0
N_FOURIER_GROUPS = 4
FOURIER_GROUP = 128
FOURIER_WIDTH = N_FOURIER_GROUPS * FOURIER_GROUP
D_FF = 2816
EPS = 1e-6
N_ADA = 6
OFF_KV = Q_LORA
OFF_KR = OFF_KV + KV_LORA
OFF_F = OFF_KR + QK_ROPE
OFF_G = OFF_F + FOURIER_WIDTH

LANES = 128
SUBLANES = 8
QK_PAD = 2 * LANES
BF16_SUBLANES = 16
V_AUG = V_HEAD + BF16_SUBLANES
VMEM_LIMIT_BYTES = 60 * 1024 * 1024

BF16 = jnp.bfloat16
F32 = jnp.float32


def _dot(a, b):
    return jnp.dot(a, b, preferred_element_type=F32)


def _dot_nt(a, b):
    return lax.dot_general(a, b, (((1,), (1,)), ((), ())), preferred_element_type=F32)


def _rms(x, g):
    return x * lax.rsqrt(jnp.mean(x * x, axis=-1, keepdims=True) + EPS) * g


def _const_spec(shape):
    nd = len(shape)
    return pl.BlockSpec(shape, lambda *_: (0,) * nd, pipeline_mode=pl.Buffered(1))


def _params(*sem):
    return pltpu.CompilerParams(dimension_semantics=sem, vmem_limit_bytes=VMEM_LIMIT_BYTES)


def _ada_kernel(c_ref, w_ref, b_ref, o_ref):
    c = c_ref[...]
    s = c * jax.nn.sigmoid(c)
    o_ref[0] = jnp.dot(s, w_ref[0], preferred_element_type=F32,
                       precision=lax.Precision.HIGHEST) + b_ref[0]


def _ada_mods(c, w_ada, b_ada):
    b = c.shape[0]
    bp = -(-b // SUBLANES) * SUBLANES
    cp = jnp.pad(c, ((0, bp - b), (0, 0)))
    out = pl.pallas_call(
        _ada_kernel,
        grid=(DEPTH, N_ADA),
        in_specs=[
            pl.BlockSpec((bp, D_MODEL), lambda l, j: (0, 0)),
            pl.BlockSpec((1, D_MODEL, D_MODEL), lambda l, j: (l, 0, j)),
            pl.BlockSpec((1, 1, D_MODEL), lambda l, j: (l, 0, j)),
        ],
        out_specs=pl.BlockSpec((1, bp, D_MODEL), lambda l, j: (l, 0, j)),
        out_shape=jax.ShapeDtypeStruct((DEPTH, bp, N_ADA * D_MODEL), F32),
        compiler_params=_params("arbitrary", "arbitrary"),
        name="ada",
    )(cp, w_ada, b_ada.reshape(DEPTH, 1, N_ADA * D_MODEL))
    return out[:, :b]


def _pre_kernel(x_ref, sh_ref, sc_ref, gpre_ref, wqa_ref, wkva_ref, wkr_ref, wf_ref, wg_ref,
                gq_ref, gkv_ref, wqn_ref, wqr_ref, wkb_ref, wvbt_ref, hsum_ref, cos_ref, sin_ref,
                q_ref, k_ref, vt_ref, u_ref, g_ref, qsq_ref, ksq_ref):
    tm = x_ref.shape[1]
    x = x_ref[0]
    h = _rms(x, gpre_ref[...]) * (1.0 + sc_ref[0]) + sh_ref[0]
    hb = h.astype(BF16)

    zq = _dot(hb, wqa_ref[...])
    zkv = _dot(hb, wkva_ref[...])
    zkr = _dot(hb, wkr_ref[...])
    cq = _rms(zq, gq_ref[...]).astype(BF16)
    ckv = _rms(zkv, gkv_ref[...]).astype(BF16)

    cos = cos_ref[...]
    sin = sin_ref[...]
    lane = lax.broadcasted_iota(jnp.int32, (tm, LANES), 1)
    low_half = (lane % QK_ROPE) < (QK_ROPE // 2)

    def rope(t):
        rot = jnp.where(low_half, pltpu.roll(t, LANES - QK_ROPE // 2, 1), pltpu.roll(t, QK_ROPE // 2, 1))
        return t * cos + rot * sin

    scale = QK_HEAD ** -0.5 * math.log2(math.e)
    kr = rope(zkr).astype(BF16)
    qn = _dot(cq, wqn_ref[...])
    qr = _dot(cq, wqr_ref[...])
    kn = _dot(ckv, wkb_ref[...])
    u_ref[0] = _dot(hb, wf_ref[...]).astype(BF16)
    g_ref[0] = _dot(hb, wg_ref[...]).astype(BF16)
    vt = _dot_nt(wvbt_ref[...], ckv)
    extra = lax.broadcasted_iota(jnp.int32, (V_AUG - V_HEAD, tm), 0)
    ones_row = jnp.where(extra == 0, 1.0, 0.0).astype(BF16)
    for hd in range(N_HEADS):
        vt_ref[0, hd * V_AUG:hd * V_AUG + V_HEAD, :] = vt[hd * V_HEAD:(hd + 1) * V_HEAD].astype(BF16)
        vt_ref[0, hd * V_AUG + V_HEAD:(hd + 1) * V_AUG, :] = ones_row

    def sq(t):
        t = t.astype(F32)
        return t * t

    kr_sq = sq(kr)
    q_sq, k_sq = [], []
    for hd in range(N_HEADS):
        lo = hd * QK_PAD
        sl = slice(hd * LANES, (hd + 1) * LANES)
        q_nope = (qn[:, sl] * scale).astype(BF16)
        pair = qr[:, (hd // 2) * LANES:(hd // 2 + 1) * LANES]
        mine = pair if hd % 2 == 0 else pltpu.roll(pair, QK_ROPE, 1)
        q_rope = (rope(jnp.where(lane < QK_ROPE, mine, 0.0)) * scale).astype(BF16)
        k_nope = kn[:, sl].astype(BF16)
        q_ref[0, :, lo:lo + LANES] = q_nope
        q_ref[0, :, lo + LANES:lo + QK_PAD] = q_rope
        k_ref[0, :, lo:lo + LANES] = k_nope
        k_ref[0, :, lo + LANES:lo + QK_PAD] = kr
        q_sq.append((sq(q_nope) + sq(q_rope)).astype(BF16))
        k_sq.append((sq(k_nope) + kr_sq).astype(BF16))
    qsq_ref[0] = _dot_nt(hsum_ref[...], jnp.concatenate(q_sq, axis=1))[:N_HEADS]
    ksq_ref[0] = _dot_nt(hsum_ref[...], jnp.concatenate(k_sq, axis=1))[:N_HEADS]


def _head_sum_matrix():
    r = jnp.arange(BF16_SUBLANES, dtype=jnp.int32)[:, None]
    c = jnp.arange(N_HEADS * LANES, dtype=jnp.int32)[None, :] // LANES
    return (r == c).astype(BF16)


def _pre_call(x, sh, sc, lw, cos_t, sin_t, tm):
    b, s, _ = x.shape
    row = lambda bi, i: (bi, i, 0)
    vec = lambda bi, i: (bi, 0, 0)
    weights = [lw["g_mix_pre"], lw["w_qa"], lw["w_kva"], lw["w_kr"], lw["w_f"], lw["w_g"],
               lw["g_q"], lw["g_kv"], lw["w_qn"], lw["w_qr"], lw["w_kb"], lw["w_vbt"], _head_sum_matrix()]
    return pl.pallas_call(
        _pre_kernel,
        grid=(b, s // tm),
        in_specs=[pl.BlockSpec((1, tm, D_MODEL), row),
                  pl.BlockSpec((1, 1, D_MODEL), vec),
                  pl.BlockSpec((1, 1, D_MODEL), vec)]
                 + [_const_spec(w.shape) for w in weights]
                 + [pl.BlockSpec((tm, LANES), lambda bi, i: (i, 0)),
                    pl.BlockSpec((tm, LANES), lambda bi, i: (i, 0))],
        out_specs=[pl.BlockSpec((1, tm, N_HEADS * QK_PAD), row),
                   pl.BlockSpec((1, tm, N_HEADS * QK_PAD), row),
                   pl.BlockSpec((1, N_HEADS * V_AUG, tm), lambda bi, i: (bi, 0, i)),
                   pl.BlockSpec((1, tm, FOURIER_WIDTH), row),
                   pl.BlockSpec((1, tm, 2 * D_MODEL), row),
                   pl.BlockSpec((1, N_HEADS, tm), lambda bi, i: (bi, 0, i)),
                   pl.BlockSpec((1, N_HEADS, tm), lambda bi, i: (bi, 0, i))],
        out_shape=[jax.ShapeDtypeStruct((b, s, N_HEADS * QK_PAD), BF16),
                   jax.ShapeDtypeStruct((b, s, N_HEADS * QK_PAD), BF16),
                   jax.ShapeDtypeStruct((b, N_HEADS * V_AUG, s), BF16),
                   jax.ShapeDtypeStruct((b, s, FOURIER_WIDTH), BF16),
                   jax.ShapeDtypeStruct((b, s, 2 * D_MODEL), BF16),
                   jax.ShapeDtypeStruct((b, N_HEADS, s), F32),
                   jax.ShapeDtypeStruct((b, N_HEADS, s), F32)],
        compiler_params=_params("parallel", "parallel"),
        name="pre",
    )(x, sh, sc, *weights, cos_t, sin_t)


DFT_N2 = 16
DFT_JG = BF16_SUBLANES
DFT_PB = DFT_N2 * BF16_SUBLANES


def _four_kernel(u_ref, perm_ref, wcs_ref, w1_ref, bd_ref, o_ref, ab_ref, z_ref):
    s = u_ref.shape[1]
    n1 = s // DFT_N2
    rows = BF16_SUBLANES
    for blk in range(s // DFT_PB):
        ub = u_ref[0, blk * DFT_PB:(blk + 1) * DFT_PB, :]
        up = _dot(perm_ref[...], ub).astype(BF16)
        for g in range(N_FOURIER_GROUPS):
            sl = slice(g * FOURIER_GROUP, (g + 1) * FOURIER_GROUP)
            ab = _dot(up[:, sl], wcs_ref[...]).astype(BF16)
            for s2 in range(DFT_N2):
                src = slice(s2 * rows, (s2 + 1) * rows)
                dst = slice(blk * rows, (blk + 1) * rows)
                ab_ref[s2, dst, sl] = ab[src, :FOURIER_GROUP]
                ab_ref[s2, n1 + blk * rows:n1 + (blk + 1) * rows, sl] = ab[src, FOURIER_GROUP:]
    kg = 2 * DFT_N2 * DFT_JG
    for s2 in range(DFT_N2):
        z = _dot(w1_ref[s2], ab_ref[s2]).astype(BF16)
        for grp in range(n1 // DFT_JG):
            for r in range(2):
                dst = grp * kg + r * DFT_N2 * DFT_JG + s2 * DFT_JG
                z_ref[dst:dst + DFT_JG, :] = z[r * n1 + grp * DFT_JG:r * n1 + (grp + 1) * DFT_JG, :]
    for grp in range(n1 // DFT_JG):
        y = _dot(bd_ref[...], z_ref[grp * kg:(grp + 1) * kg, :]).astype(BF16)
        for j2 in range(DFT_N2):
            dst = n1 * j2 + grp * DFT_JG
            o_ref[0, dst:dst + DFT_JG, :] = y[j2 * DFT_JG:(j2 + 1) * DFT_JG, :]


def _four_call(u, tables):
    b, s, _ = u.shape
    n1 = s // DFT_N2
    return pl.pallas_call(
        _four_kernel,
        grid=(b,),
        in_specs=[pl.BlockSpec((1, s, FOURIER_WIDTH), lambda bi: (bi, 0, 0))]
                 + [_const_spec(t.shape) for t in tables],
        out_specs=pl.BlockSpec((1, s, FOURIER_WIDTH), lambda bi: (bi, 0, 0)),
        out_shape=jax.ShapeDtypeStruct((b, s, FOURIER_WIDTH), BF16),
        scratch_shapes=[pltpu.VMEM((DFT_N2, 2 * n1, FOURIER_WIDTH), BF16),
                        pltpu.VMEM((2 * s, FOURIER_WIDTH), BF16)],
        compiler_params=_params("parallel"),
        name="four",
    )(u, *tables)


ATTN_MIN_SUM = 2.0 ** -64
ATTN_BOUND_SLACK = 1.0 + 2.0 ** -6


ATTN_STEP_KEY_ROWS = 4096


def _attn_kernel(q_ref, k_ref, vt_ref, qsq_ref, ksq_ref, o_ref, *, tq, heads):
    nq = q_ref.shape[1] // tq

    def tile(hh, cols, m_of):
        qk = slice(hh * QK_PAD, (hh + 1) * QK_PAD)
        st = _dot_nt(k_ref[0, :, qk], q_ref[0, cols, qk])
        p = jnp.exp2(st - m_of(st))
        ot = _dot(vt_ref[0, hh * V_AUG:(hh + 1) * V_AUG, :], p.astype(BF16))
        l = ot[V_HEAD:V_HEAD + 1]
        o_ref[0, cols, hh * V_HEAD:(hh + 1) * V_HEAD] = (ot[:V_HEAD] * (1.0 / l)).T.astype(BF16)
        return l

    l_min = None
    for hh in range(heads):
        k_max = jnp.sqrt(jnp.max(ksq_ref[0, hh], axis=-1, keepdims=True)) * ATTN_BOUND_SLACK
        for j in range(nq):
            cols = slice(j * tq, (j + 1) * tq)
            bound = jnp.sqrt(qsq_ref[0, hh, :, cols]) * k_max
            l = tile(hh, cols, lambda st: bound)
            l_min = l if l_min is None else jnp.minimum(l_min, l)

    @pl.when(jnp.logical_not(jnp.min(l_min) >= ATTN_MIN_SUM))
    def _():
        for hh in range(heads):
            def body(j, carry):
                cols = pl.ds(pl.multiple_of(j * tq, tq), tq)
                tile(hh, cols, lambda st: jnp.max(st, axis=0, keepdims=True))
                return carry
            lax.fori_loop(0, nq, body, 0)


def _attn_call(q, k, vt, qsq, ksq, tq):
    b, s, _ = q.shape
    hps = max(1, min(2, ATTN_STEP_KEY_ROWS // s))
    norm_spec = pl.BlockSpec((1, hps, 1, s), lambda bi, h: (bi, h, 0, 0))
    return pl.pallas_call(
        functools.partial(_attn_kernel, tq=tq, heads=hps),
        grid=(b, N_HEADS // hps),
        in_specs=[pl.BlockSpec((1, s, hps * QK_PAD), lambda bi, h: (bi, 0, h)),
                  pl.BlockSpec((1, s, hps * QK_PAD), lambda bi, h: (bi, 0, h)),
                  pl.BlockSpec((1, hps * V_AUG, s), lambda bi, h: (bi, h, 0)),
                  norm_spec, norm_spec],
        out_specs=pl.BlockSpec((1, s, hps * V_HEAD), lambda bi, h: (bi, 0, h)),
        out_shape=jax.ShapeDtypeStruct((b, s, N_HEADS * V_HEAD), BF16),
        compiler_params=_params("parallel", "parallel"),
        name="attn",
    )(q, k, vt, qsq.reshape(b, N_HEADS, 1, s), ksq.reshape(b, N_HEADS, 1, s))


def _merge_kernel(a_ref, f_ref, g_ref, x_ref, gt_ref, gpost_ref, wao_ref, wfo_ref, wout_ref, o_ref):
    half = x_ref.shape[1] // 2
    halves = (slice(0, half), slice(half, 2 * half))
    branches = [(_dot(a_ref[0, rows, :], wao_ref[...]), _dot(f_ref[0, rows, :], wfo_ref[...])) for rows in halves]
    ys = []
    for rows, (branch_a, branch_b) in zip(halves, branches):
        gate_a = g_ref[0, rows, :D_MODEL].astype(F32)
        gate_b = g_ref[0, rows, D_MODEL:].astype(F32)
        merged = jax.nn.sigmoid(gate_a) * branch_a + jax.nn.sigmoid(gate_b) * branch_b
        ys.append(_dot(merged.astype(BF16), wout_ref[...]))
    for rows, y in zip(halves, ys):
        o_ref[0, rows, :] = x_ref[0, rows, :] + gt_ref[0] * _rms(y, gpost_ref[...])


def _merge_call(attn, four, gates, x, gt, lw, tm):
    b, s, _ = x.shape
    row = lambda bi, i: (bi, i, 0)
    weights = [lw["g_mix_post"], lw["w_attn_o"], lw["w_four"], lw["w_out"]]
    return pl.pallas_call(
        _merge_kernel,
        grid=(b, s // tm),
        in_specs=[pl.BlockSpec((1, tm, N_HEADS * V_HEAD), row),
                  pl.BlockSpec((1, tm, FOURIER_WIDTH), row),
                  pl.BlockSpec((1, tm, 2 * D_MODEL), row),
                  pl.BlockSpec((1, tm, D_MODEL), row),
                  pl.BlockSpec((1, 1, D_MODEL), lambda bi, i: (bi, 0, 0))]
                 + [_const_spec(w.shape) for w in weights],
        out_specs=pl.BlockSpec((1, tm, D_MODEL), row),
        out_shape=jax.ShapeDtypeStruct(x.shape, F32),
        compiler_params=_params("parallel", "parallel"),
        name="merge",
    )(attn, four, gates, x, gt, *weights)


FFN_CHUNK = 2 * LANES
assert D_FF % FFN_CHUNK == 0
GELU_C = math.sqrt(2.0 / math.pi)


def _gelu_tanh_x2(a):
    return a * (1.0 + jnp.tanh(a * (GELU_C + (GELU_C * 0.044715) * (a * a))))


def _ffn_kernel(xp_ref, x_ref, xn_ref, sh_ref, sc_ref, gt_ref, gpre_ref, gpost_ref,
                wua_ref, wub_ref, wca_ref, wcb_ref, bca_ref, bcb_ref, wd_ref, o_ref, act_ref):
    i = pl.program_id(1)
    last = pl.num_programs(1) - 1
    tm = x_ref.shape[1]
    te = tm + 2 * SUBLANES
    x = x_ref[0]
    xe = jnp.concatenate([xp_ref[0], x, xn_ref[0]], axis=0)
    he = _rms(xe, gpre_ref[...]) * (1.0 + sc_ref[0]) + sh_ref[0]
    r = lax.broadcasted_iota(jnp.int32, (te, 1), 0)
    inside = jnp.logical_and(jnp.logical_or(r >= SUBLANES, i > 0),
                             jnp.logical_or(r < tm + SUBLANES, i < last))
    he = jnp.where(inside, he, 0.0).astype(BF16)

    def conv(u, w_ref, b_ref, cs):
        prev = pltpu.roll(u, 1, 0)[SUBLANES:SUBLANES + tm]
        nxt = pltpu.roll(u, te - 1, 0)[SUBLANES:SUBLANES + tm]
        mid = u[SUBLANES:SUBLANES + tm]
        return prev * w_ref[0:1, cs] + mid * w_ref[1:2, cs] + nxt * w_ref[2:3, cs] + b_ref[:, cs]

    for c in range(D_FF // FFN_CHUNK):
        cs = slice(c * FFN_CHUNK, (c + 1) * FFN_CHUNK)
        a = conv(_dot(he, wua_ref[:, cs]), wca_ref, bca_ref, cs)
        g = conv(_dot(he, wub_ref[:, cs]), wcb_ref, bcb_ref, cs)
        act_ref[:, cs] = (_gelu_tanh_x2(a) * g).astype(BF16)

    y = _dot(act_ref[...], wd_ref[...])
    o_ref[0] = x + gt_ref[0] * _rms(y, gpost_ref[...])


def _ffn_call(x, sh, sc, gt, lw, tm):
    b, s, _ = x.shape
    nb = tm // SUBLANES
    row = lambda bi, i: (bi, i, 0)
    vec = lambda bi, i: (bi, 0, 0)
    weights = [lw["g_ffn_pre"], lw["g_ffn_post"], lw["w_up_a"], lw["w_up_b"],
               lw["w_conv_a"], lw["w_conv_b"], lw["b_conv_a"], lw["b_conv_b"], lw["w_down"]]
    return pl.pallas_call(
        _ffn_kernel,
        grid=(b, s // tm),
        in_specs=[pl.BlockSpec((1, SUBLANES, D_MODEL), lambda bi, i: (bi, jnp.maximum(i * nb - 1, 0), 0)),
                  pl.BlockSpec((1, tm, D_MODEL), row),
                  pl.BlockSpec((1, SUBLANES, D_MODEL),
                               lambda bi, i: (bi, jnp.minimum((i + 1) * nb, s // SUBLANES - 1), 0)),
                  pl.BlockSpec((1, 1, D_MODEL), vec),
                  pl.BlockSpec((1, 1, D_MODEL), vec),
                  pl.BlockSpec((1, 1, D_MODEL), vec)]
                 + [_const_spec(w.shape) for w in weights],
        out_specs=pl.BlockSpec((1, tm, D_MODEL), row),
        out_shape=jax.ShapeDtypeStruct(x.shape, F32),
        scratch_shapes=[pltpu.VMEM((tm, D_FF), BF16)],
        compiler_params=_params("parallel", "parallel"),
        name="ffn",
    )(x, x, x, sh, sc, gt, *weights)


def _rope_tables(s):
    half = QK_ROPE // 2
    inv = 1.0 / (ROPE_BASE ** (jnp.arange(half, dtype=F32) / half))
    ang = jnp.arange(s, dtype=F32)[:, None] * inv[None, :]
    cos, sin = jnp.cos(ang), jnp.sin(ang)
    zero = jnp.zeros((s, LANES - QK_ROPE), F32)
    return (jnp.concatenate([cos, cos, zero], axis=1),
            jnp.concatenate([-sin, sin, zero], axis=1))


def _dft_cos_sin(n):
    idx = jnp.arange(n, dtype=jnp.int32)
    ang = ((idx[:, None] * idx[None, :]) % n).astype(F32) * (2.0 * math.pi / n)
    norm = n ** -0.5
    return jnp.cos(ang) * norm, jnp.sin(ang) * norm


def _fourier_tables(s):
    n1 = s // DFT_N2
    rows = BF16_SUBLANES
    new = jnp.arange(DFT_PB, dtype=jnp.int32)
    old = DFT_N2 * (new % rows) + new // rows
    perm = (old[:, None] == jnp.arange(DFT_PB, dtype=jnp.int32)[None, :]).astype(BF16)
    cc, sc_ = _dft_cos_sin(FOURIER_GROUP)
    wcs = jnp.concatenate([cc, sc_], axis=1).astype(BF16)
    j1 = jnp.arange(n1, dtype=jnp.int32)[None, :, None]
    s1 = jnp.arange(n1, dtype=jnp.int32)[None, None, :]
    s2 = jnp.arange(DFT_N2, dtype=jnp.int32)[:, None, None]
    ang = ((j1 * (DFT_N2 * s1 + s2)) % s).astype(F32) * (2.0 * math.pi / s)
    c1, sn1 = jnp.cos(ang) * s ** -0.5, jnp.sin(ang) * s ** -0.5
    w1 = jnp.concatenate([jnp.concatenate([c1, -sn1], axis=2),
                          jnp.concatenate([sn1, c1], axis=2)], axis=1).astype(BF16)
    j2 = jnp.arange(DFT_N2, dtype=jnp.int32)
    ang2 = ((j2[:, None] * j2[None, :]) % DFT_N2).astype(F32) * (2.0 * math.pi / DFT_N2)
    eye = jnp.eye(DFT_JG, dtype=F32)
    bd = jnp.concatenate([jnp.kron(jnp.cos(ang2), eye), jnp.kron(-jnp.sin(ang2), eye)], axis=1).astype(BF16)
    return perm, wcs, w1, bd


def _layer_weights(l, g_mix_pre, g_mix_post, w_in, g_q, w_q_b, g_kv, w_kv_b, w_attn_o, w_four, w_out,
                   g_ffn_pre, g_ffn_post, w_up, w_conv, b_conv, w_down):
    wi = w_in[l]
    wqb = w_q_b[l].reshape(Q_LORA, N_HEADS, QK_HEAD)
    wkvb = w_kv_b[l].reshape(KV_LORA, N_HEADS, QK_NOPE + V_HEAD)
    return {
        "g_mix_pre": g_mix_pre[l].reshape(1, D_MODEL),
        "g_mix_post": g_mix_post[l].reshape(1, D_MODEL),
        "w_qa": wi[:, :OFF_KV].astype(BF16),
        "w_kva": wi[:, OFF_KV:OFF_KR].astype(BF16),
        "w_kr": jnp.pad(wi[:, OFF_KR:OFF_F], ((0, 0), (0, LANES - QK_ROPE))).astype(BF16),
        "w_f": wi[:, OFF_F:OFF_G].astype(BF16),
        "w_g": wi[:, OFF_G:].astype(BF16),
        "g_q": g_q[l].reshape(1, Q_LORA),
        "g_kv": g_kv[l].reshape(1, KV_LORA),
        "w_qn": wqb[:, :, :QK_NOPE].reshape(Q_LORA, N_HEADS * QK_NOPE).astype(BF16),
        "w_qr": wqb[:, :, QK_NOPE:].reshape(Q_LORA, N_HEADS * QK_ROPE).astype(BF16),
        "w_kb": wkvb[:, :, :QK_NOPE].reshape(KV_LORA, N_HEADS * QK_NOPE).astype(BF16),
        "w_vbt": wkvb[:, :, QK_NOPE:].reshape(KV_LORA, N_HEADS * V_HEAD).T.astype(BF16),
        "w_attn_o": w_attn_o[l].astype(BF16),
        "w_four": w_four[l].astype(BF16),
        "w_out": w_out[l].astype(BF16),
        "g_ffn_pre": g_ffn_pre[l].reshape(1, D_MODEL),
        "g_ffn_post": g_ffn_post[l].reshape(1, D_MODEL),
        "w_up_a": w_up[l][:, :D_FF].astype(BF16),
        "w_up_b": w_up[l][:, D_FF:].astype(BF16),
        "w_conv_a": w_conv[l][:, :D_FF],
        "w_conv_b": 0.5 * w_conv[l][:, D_FF:],
        "b_conv_a": b_conv[l][:D_FF].reshape(1, D_FF),
        "b_conv_b": 0.5 * b_conv[l][D_FF:].reshape(1, D_FF),
        "w_down": w_down[l].astype(BF16),
    }


def _tile(s, want):
    t = min(s, want)
    assert s % t == 0
    return t


def _trunk(x, c, w_ada, b_ada, layers):
    b, s, _ = x.shape
    tm = _tile(s, 512)
    tm_big = _tile(s, 1024)
    tq = _tile(s, 512)
    assert s % DFT_PB == 0 and (s // DFT_N2) % DFT_JG == 0
    cos_t, sin_t = _rope_tables(s)
    four_tables = _fourier_tables(s)
    mods = _ada_mods(c, w_ada, b_ada).reshape(DEPTH, b, N_ADA, 1, D_MODEL)
    for l, lw in enumerate(layers):
        sh1, sc1, gt1, sh2, sc2, gt2 = [mods[l, :, j] for j in range(N_ADA)]
        q, k, vt, u, gates, qsq, ksq = _pre_call(x, sh1, sc1, lw, cos_t, sin_t, tm)
        four = _four_call(u, four_tables)
        attn = _attn_call(q, k, vt, qsq, ksq, tq)
        x = _merge_call(attn, four, gates, x, gt1, lw, tm_big)
        x = _ffn_call(x, sh2, sc2, gt2, lw, tm_big)
    return x


def kernel(x_prompt, x_sample, c_prompt, c_sample, w_ada, b_ada, g_mix_pre, g_mix_post, w_in, g_q, w_q_b,
           g_kv, w_kv_b, w_attn_o, w_four, w_out, g_ffn_pre, g_ffn_post, w_up, w_conv, b_conv, w_down):
    layers = [_layer_weights(l, g_mix_pre, g_mix_post, w_in, g_q, w_q_b, g_kv, w_kv_b, w_attn_o, w_four,
                             w_out, g_ffn_pre, g_ffn_post, w_up, w_conv, b_conv, w_down)
              for l in range(DEPTH)]
    y_prompt = _trunk(x_prompt, c_prompt, w_ada, b_ada, layers)
    y_sample = _trunk(x_sample, c_sample, w_ada, b_ada, layers)
    return (y_prompt, y_sample)
```

```python
import functools
import math

import jax
import jax.numpy as jnp
from jax import lax
from jax.experimental import pallas as pl
from jax.experimental.pallas import tpu as pltpu

D_MODEL = 1024
DEPTH = 4
N_HEADS = 8
N_HEAD_PAIRS = N_HEADS // 2
QK_NOPE = 128
QK_ROPE = 64
V_HEAD = 128
Q_LORA = 512
KV_LORA = 256
QK_HEAD = QK_NOPE + QK_ROPE
ROPE_BASE = 10000.0
N_FOURIER_GROUPS = 4
FOURIER_GROUP = 128
FOURIER_WIDTH = N_FOURIER_GROUPS * FOURIER_GROUP
D_FF = 2816
EPS = 1e-6
N_ADA = 6
OFF_KV = Q_LORA
OFF_KR = OFF_KV + KV_LORA
OFF_F = OFF_KR + QK_ROPE
OFF_G = OFF_F + FOURIER_WIDTH

LANES = 128
SUBLANES = 8
QK_PAD = 2 * LANES
BF16_SUBLANES = 16
V_AUG = V_HEAD + BF16_SUBLANES
VMEM_LIMIT_BYTES = 60 * 1024 * 1024

BF16 = jnp.bfloat16
F32 = jnp.float32


def _dot(a, b):
    return jnp.dot(a, b, preferred_element_type=F32)


def _dot_nt(a, b):
    return lax.dot_general(a, b, (((1,), (1,)), ((), ())), preferred_element_type=F32)


def _rms(x, g):
    return x * lax.rsqrt(jnp.mean(x * x, axis=-1, keepdims=True) + EPS) * g


def _const_spec(shape):
    nd = len(shape)
    return pl.BlockSpec(shape, lambda *_: (0,) * nd, pipeline_mode=pl.Buffered(1))


def _params(*sem):
    return pltpu.CompilerParams(dimension_semantics=sem, vmem_limit_bytes=VMEM_LIMIT_BYTES)


def _ada_kernel(c_ref, w_ref, b_ref, o_ref):
    c = c_ref[...]
    s = c * jax.nn.sigmoid(c)
    o_ref[0] = jnp.dot(s, w_ref[0], preferred_element_type=F32,
                       precision=lax.Precision.HIGHEST) + b_ref[0]


def _ada_mods(c, w_ada, b_ada):
    b = c.shape[0]
    bp = -(-b // SUBLANES) * SUBLANES
    cp = jnp.pad(c, ((0, bp - b), (0, 0)))
    out = pl.pallas_call(
        _ada_kernel,
        grid=(DEPTH, N_ADA),
        in_specs=[
            pl.BlockSpec((bp, D_MODEL), lambda l, j: (0, 0)),
            pl.BlockSpec((1, D_MODEL, D_MODEL), lambda l, j: (l, 0, j)),
            pl.BlockSpec((1, 1, D_MODEL), lambda l, j: (l, 0, j)),
        ],
        out_specs=pl.BlockSpec((1, bp, D_MODEL), lambda l, j: (l, 0, j)),
        out_shape=jax.ShapeDtypeStruct((DEPTH, bp, N_ADA * D_MODEL), F32),
        compiler_params=_params("arbitrary", "arbitrary"),
        name="ada",
    )(cp, w_ada, b_ada.reshape(DEPTH, 1, N_ADA * D_MODEL))
    return out[:, :b]


def _pre_kernel(x_ref, sh_ref, sc_ref, gpre_ref, wqa_ref, wkva_ref, wkr_ref, wf_ref, wg_ref,
                gq_ref, gkv_ref, wqn_ref, wqr_ref, wkb_ref, wvbt_ref, hsum_ref, cos_ref, sin_ref,
                q_ref, k_ref, vt_ref, u_ref, g_ref, qsq_ref, ksq_ref):
    tm = x_ref.shape[1]
    x = x_ref[0]
    h = _rms(x, gpre_ref[...]) * (1.0 + sc_ref[0]) + sh_ref[0]
    hb = h.astype(BF16)

    zq = _dot(hb, wqa_ref[...])
    zkv = _dot(hb, wkva_ref[...])
    zkr = _dot(hb, wkr_ref[...])
    cq = _rms(zq, gq_ref[...]).astype(BF16)
    ckv = _rms(zkv, gkv_ref[...]).astype(BF16)

    cos = cos_ref[...]
    sin = sin_ref[...]
    lane = lax.broadcasted_iota(jnp.int32, (tm, LANES), 1)
    low_half = (lane % QK_ROPE) < (QK_ROPE // 2)

    def rope(t):
        rot = jnp.where(low_half, pltpu.roll(t, LANES - QK_ROPE // 2, 1), pltpu.roll(t, QK_ROPE // 2, 1))
        return t * cos + rot * sin

    scale = QK_HEAD ** -0.5 * math.log2(math.e)
    kr = rope(zkr).astype(BF16)
    qn = _dot(cq, wqn_ref[...])
    qr = _dot(cq, wqr_ref[...])
    kn = _dot(ckv, wkb_ref[...])
    u_ref[0] = _dot(hb, wf_ref[...]).astype(BF16)
    g_ref[0] = _dot(hb, wg_ref[...]).astype(BF16)
    vt = _dot_nt(wvbt_ref[...], ckv)
    extra = lax.broadcasted_iota(jnp.int32, (V_AUG - V_HEAD, tm), 0)
    ones_row = jnp.where(extra == 0, 1.0, 0.0).astype(BF16)
    for hd in range(N_HEADS):
        vt_ref[0, hd * V_AUG:hd * V_AUG + V_HEAD, :] = vt[hd * V_HEAD:(hd + 1) * V_HEAD].astype(BF16)
        vt_ref[0, hd * V_AUG + V_HEAD:(hd + 1) * V_AUG, :] = ones_row

    def sq(t):
        t = t.astype(F32)
        return t * t

    kr_sq = sq(kr)
    q_sq, k_sq = [], []
    for hd in range(N_HEADS):
        lo = hd * QK_PAD
        sl = slice(hd * LANES, (hd + 1) * LANES)
        q_nope = (qn[:, sl] * scale).astype(BF16)
        pair = qr[:, (hd // 2) * LANES:(hd // 2 + 1) * LANES]
        mine = pair if hd % 2 == 0 else pltpu.roll(pair, QK_ROPE, 1)
        q_rope = (rope(jnp.where(lane < QK_ROPE, mine, 0.0)) * scale).astype(BF16)
        k_nope = kn[:, sl].astype(BF16)
        q_ref[0, :, lo:lo + LANES] = q_nope
        q_ref[0, :, lo + LANES:lo + QK_PAD] = q_rope
        k_ref[0, :, lo:lo + LANES] = k_nope
        k_ref[0, :, lo + LANES:lo + QK_PAD] = kr
        q_sq.append(sq(q_nope) + sq(q_rope))
        k_sq.append(sq(k_nope) + kr_sq)

    def pair_sums(parts):
        return jnp.concatenate([(parts[2 * p] + parts[2 * p + 1]).astype(BF16) for p in range(N_HEAD_PAIRS)], axis=1)

    qsq_ref[0] = _dot_nt(hsum_ref[...], pair_sums(q_sq))[:N_HEAD_PAIRS]
    ksq_ref[0] = _dot_nt(hsum_ref[...], pair_sums(k_sq))[:N_HEAD_PAIRS]


def _head_sum_matrix():
    r = jnp.arange(BF16_SUBLANES, dtype=jnp.int32)[:, None]
    c = jnp.arange(N_HEAD_PAIRS * LANES, dtype=jnp.int32)[None, :] // LANES
    return (r == c).astype(BF16)


def _pre_call(x, sh, sc, lw, cos_t, sin_t, tm):
    b, s, _ = x.shape
    row = lambda bi, i: (bi, i, 0)
    vec = lambda bi, i: (bi, 0, 0)
    weights = [lw["g_mix_pre"], lw["w_qa"], lw["w_kva"], lw["w_kr"], lw["w_f"], lw["w_g"],
               lw["g_q"], lw["g_kv"], lw["w_qn"], lw["w_qr"], lw["w_kb"], lw["w_vbt"], _head_sum_matrix()]
    return pl.pallas_call(
        _pre_kernel,
        grid=(b, s // tm),
        in_specs=[pl.BlockSpec((1, tm, D_MODEL), row),
                  pl.BlockSpec((1, 1, D_MODEL), vec),
                  pl.BlockSpec((1, 1, D_MODEL), vec)]
                 + [_const_spec(w.shape) for w in weights]
                 + [pl.BlockSpec((tm, LANES), lambda bi, i: (i, 0)),
                    pl.BlockSpec((tm, LANES), lambda bi, i: (i, 0))],
        out_specs=[pl.BlockSpec((1, tm, N_HEADS * QK_PAD), row),
                   pl.BlockSpec((1, tm, N_HEADS * QK_PAD), row),
                   pl.BlockSpec((1, N_HEADS * V_AUG, tm), lambda bi, i: (bi, 0, i)),
                   pl.BlockSpec((1, tm, FOURIER_WIDTH), row),
                   pl.BlockSpec((1, tm, 2 * D_MODEL), row),
                   pl.BlockSpec((1, N_HEAD_PAIRS, tm), lambda bi, i: (bi, 0, i)),
                   pl.BlockSpec((1, N_HEAD_PAIRS, tm), lambda bi, i: (bi, 0, i))],
        out_shape=[jax.ShapeDtypeStruct((b, s, N_HEADS * QK_PAD), BF16),
                   jax.ShapeDtypeStruct((b, s, N_HEADS * QK_PAD), BF16),
                   jax.ShapeDtypeStruct((b, N_HEADS * V_AUG, s), BF16),
                   jax.ShapeDtypeStruct((b, s, FOURIER_WIDTH), BF16),
                   jax.ShapeDtypeStruct((b, s, 2 * D_MODEL), BF16),
                   jax.ShapeDtypeStruct((b, N_HEAD_PAIRS, s), F32),
                   jax.ShapeDtypeStruct((b, N_HEAD_PAIRS, s), F32)],
        compiler_params=_params("parallel", "parallel"),
        name="pre",
    )(x, sh, sc, *weights, cos_t, sin_t)


DFT_N2 = 16
DFT_JG = BF16_SUBLANES
DFT_PB = DFT_N2 * BF16_SUBLANES


def _four_kernel(u_ref, perm_ref, wcs_ref, w1_ref, bd_ref, o_ref, ab_ref, z_ref):
    s = u_ref.shape[1]
    n1 = s // DFT_N2
    rows = BF16_SUBLANES
    for blk in range(s // DFT_PB):
        ub = u_ref[0, blk * DFT_PB:(blk + 1) * DFT_PB, :]
        up = _dot(perm_ref[...], ub).astype(BF16)
        for g in range(N_FOURIER_GROUPS):
            sl = slice(g * FOURIER_GROUP, (g + 1) * FOURIER_GROUP)
            ab = _dot(up[:, sl], wcs_ref[...]).astype(BF16)
            for s2 in range(DFT_N2):
                src = slice(s2 * rows, (s2 + 1) * rows)
                dst = slice(blk * rows, (blk + 1) * rows)
                ab_ref[s2, dst, sl] = ab[src, :FOURIER_GROUP]
                ab_ref[s2, n1 + blk * rows:n1 + (blk + 1) * rows, sl] = ab[src, FOURIER_GROUP:]
    kg = 2 * DFT_N2 * DFT_JG
    for s2 in range(DFT_N2):
        z = _dot(w1_ref[s2], ab_ref[s2]).astype(BF16)
        for grp in range(n1 // DFT_JG):
            for r in range(2):
                dst = grp * kg + r * DFT_N2 * DFT_JG + s2 * DFT_JG
                z_ref[dst:dst + DFT_JG, :] = z[r * n1 + grp * DFT_JG:r * n1 + (grp + 1) * DFT_JG, :]
    for grp in range(n1 // DFT_JG):
        y = _dot(bd_ref[...], z_ref[grp * kg:(grp + 1) * kg, :]).astype(BF16)
        for j2 in range(DFT_N2):
            dst = n1 * j2 + grp * DFT_JG
            o_ref[0, dst:dst + DFT_JG, :] = y[j2 * DFT_JG:(j2 + 1) * DFT_JG, :]


def _four_call(u, tables):
    b, s, _ = u.shape
    n1 = s // DFT_N2
    return pl.pallas_call(
        _four_kernel,
        grid=(b,),
        in_specs=[pl.BlockSpec((1, s, FOURIER_WIDTH), lambda bi: (bi, 0, 0))]
                 + [_const_spec(t.shape) for t in tables],
        out_specs=pl.BlockSpec((1, s, FOURIER_WIDTH), lambda bi: (bi, 0, 0)),
        out_shape=jax.ShapeDtypeStruct((b, s, FOURIER_WIDTH), BF16),
        scratch_shapes=[pltpu.VMEM((DFT_N2, 2 * n1, FOURIER_WIDTH), BF16),
                        pltpu.VMEM((2 * s, FOURIER_WIDTH), BF16)],
        compiler_params=_params("parallel"),
        name="four",
    )(u, *tables)


ATTN_MIN_SUM = 2.0 ** -64
ATTN_BOUND_SLACK = 1.0 + 2.0 ** -6


ATTN_STEP_KEY_ROWS = 4096


def _attn_kernel(q_ref, k_ref, vt_ref, qsq_ref, ksq_ref, o_ref, *, tq, heads):
    nq = q_ref.shape[1] // tq

    def tile(hh, cols, m_of):
        qk = slice(hh * QK_PAD, (hh + 1) * QK_PAD)
        st = _dot_nt(k_ref[0, :, qk], q_ref[0, cols, qk])
        p = jnp.exp2(st - m_of(st))
        ot = _dot(vt_ref[0, hh * V_AUG:(hh + 1) * V_AUG, :], p.astype(BF16))
        l = ot[V_HEAD:V_HEAD + 1]
        o_ref[0, cols, hh * V_HEAD:(hh + 1) * V_HEAD] = (ot[:V_HEAD] * (1.0 / l)).T.astype(BF16)
        return l

    l_min = None
    k_max = jnp.sqrt(jnp.max(ksq_ref[0, 0], axis=-1, keepdims=True)) * ATTN_BOUND_SLACK
    for hh in range(heads):
        for j in range(nq):
            cols = slice(j * tq, (j + 1) * tq)
            bound = jnp.sqrt(qsq_ref[0, 0, :, cols]) * k_max
            l = tile(hh, cols, lambda st: bound)
            l_min = l if l_min is None else jnp.minimum(l_min, l)

    @pl.when(jnp.logical_not(jnp.min(l_min) >= ATTN_MIN_SUM))
    def _():
        for hh in range(heads):
            def body(j, carry):
                cols = pl.ds(pl.multiple_of(j * tq, tq), tq)
                tile(hh, cols, lambda st: jnp.max(st, axis=0, keepdims=True))
                return carry
            lax.fori_loop(0, nq, body, 0)


def _attn_call(q, k, vt, qsq, ksq, tq):
    b, s, _ = q.shape
    hps = max(1, min(2, ATTN_STEP_KEY_ROWS // s))
    norm_spec = pl.BlockSpec((1, 1, 1, s), lambda bi, h: (bi, (h * hps) // 2, 0, 0))
    return pl.pallas_call(
        functools.partial(_attn_kernel, tq=tq, heads=hps),
        grid=(b, N_HEADS // hps),
        in_specs=[pl.BlockSpec((1, s, hps * QK_PAD), lambda bi, h: (bi, 0, h)),
                  pl.BlockSpec((1, s, hps * QK_PAD), lambda bi, h: (bi, 0, h)),
                  pl.BlockSpec((1, hps * V_AUG, s), lambda bi, h: (bi, h, 0)),
                  norm_spec, norm_spec],
        out_specs=pl.BlockSpec((1, s, hps * V_HEAD), lambda bi, h: (bi, 0, h)),
        out_shape=jax.ShapeDtypeStruct((b, s, N_HEADS * V_HEAD), BF16),
        compiler_params=_params("parallel", "parallel"),
        name="attn",
    )(q, k, vt, qsq.reshape(b, N_HEAD_PAIRS, 1, s), ksq.reshape(b, N_HEAD_PAIRS, 1, s))


def _merge_kernel(a_ref, f_ref, g_ref, x_ref, gt_ref, gpost_ref, wao_ref, wfo_ref, wout_ref, o_ref):
    half = x_ref.shape[1] // 2
    halves = (slice(0, half), slice(half, 2 * half))
    branches = [(_dot(a_ref[0, rows, :], wao_ref[...]), _dot(f_ref[0, rows, :], wfo_ref[...])) for rows in halves]
    ys = []
    for rows, (branch_a, branch_b) in zip(halves, branches):
        gate_a = g_ref[0, rows, :D_MODEL].astype(F32)
        gate_b = g_ref[0, rows, D_MODEL:].astype(F32)
        merged = jax.nn.sigmoid(gate_a) * branch_a + jax.nn.sigmoid(gate_b) * branch_b
        ys.append(_dot(merged.astype(BF16), wout_ref[...]))
    for rows, y in zip(halves, ys):
        o_ref[0, rows, :] = x_ref[0, rows, :] + gt_ref[0] * _rms(y, gpost_ref[...])


def _merge_call(attn, four, gates, x, gt, lw, tm):
    b, s, _ = x.shape
    row = lambda bi, i: (bi, i, 0)
    weights = [lw["g_mix_post"], lw["w_attn_o"], lw["w_four"], lw["w_out"]]
    return pl.pallas_call(
        _merge_kernel,
        grid=(b, s // tm),
        in_specs=[pl.BlockSpec((1, tm, N_HEADS * V_HEAD), row),
                  pl.BlockSpec((1, tm, FOURIER_WIDTH), row),
                  pl.BlockSpec((1, tm, 2 * D_MODEL), row),
                  pl.BlockSpec((1, tm, D_MODEL), row),
                  pl.BlockSpec((1, 1, D_MODEL), lambda bi, i: (bi, 0, 0))]
                 + [_const_spec(w.shape) for w in weights],
        out_specs=pl.BlockSpec((1, tm, D_MODEL), row),
        out_shape=jax.ShapeDtypeStruct(x.shape, F32),
        compiler_params=_params("parallel", "parallel"),
        name="merge",
    )(attn, four, gates, x, gt, *weights)


FFN_CHUNK = 2 * LANES
assert D_FF % FFN_CHUNK == 0
GELU_C = math.sqrt(2.0 / math.pi)


def _gelu_tanh_x2(a):
    return a * (1.0 + jnp.tanh(a * (GELU_C + (GELU_C * 0.044715) * (a * a))))


def _ffn_kernel(xp_ref, x_ref, xn_ref, sh_ref, sc_ref, gt_ref, gpre_ref, gpost_ref,
                wua_ref, wub_ref, wca_ref, wcb_ref, bca_ref, bcb_ref, wd_ref, o_ref, act_ref):
    i = pl.program_id(1)
    last = pl.num_programs(1) - 1
    tm = x_ref.shape[1]
    te = tm + 2 * SUBLANES
    x = x_ref[0]
    xe = jnp.concatenate([xp_ref[0], x, xn_ref[0]], axis=0)
    he = _rms(xe, gpre_ref[...]) * (1.0 + sc_ref[0]) + sh_ref[0]
    r = lax.broadcasted_iota(jnp.int32, (te, 1), 0)
    inside = jnp.logical_and(jnp.logical_or(r >= SUBLANES, i > 0),
                             jnp.logical_or(r < tm + SUBLANES, i < last))
    he = jnp.where(inside, he, 0.0).astype(BF16)

    def conv(u, w_ref, b_ref, cs):
        prev = pltpu.roll(u, 1, 0)[SUBLANES:SUBLANES + tm]
        nxt = pltpu.roll(u, te - 1, 0)[SUBLANES:SUBLANES + tm]
        mid = u[SUBLANES:SUBLANES + tm]
        return prev * w_ref[0:1, cs] + mid * w_ref[1:2, cs] + nxt * w_ref[2:3, cs] + b_ref[:, cs]

    for c in range(D_FF // FFN_CHUNK):
        cs = slice(c * FFN_CHUNK, (c + 1) * FFN_CHUNK)
        a = conv(_dot(he, wua_ref[:, cs]), wca_ref, bca_ref, cs)
        g = conv(_dot(he, wub_ref[:, cs]), wcb_ref, bcb_ref, cs)
        act_ref[:, cs] = (_gelu_tanh_x2(a) * g).astype(BF16)

    y = _dot(act_ref[...], wd_ref[...])
    o_ref[0] = x + gt_ref[0] * _rms(y, gpost_ref[...])


def _ffn_call(x, sh, sc, gt, lw, tm):
    b, s, _ = x.shape
    nb = tm // SUBLANES
    row = lambda bi, i: (bi, i, 0)
    vec = lambda bi, i: (bi, 0, 0)
    weights = [lw["g_ffn_pre"], lw["g_ffn_post"], lw["w_up_a"], lw["w_up_b"],
               lw["w_conv_a"], lw["w_conv_b"], lw["b_conv_a"], lw["b_conv_b"], lw["w_down"]]
    return pl.pallas_call(
        _ffn_kernel,
        grid=(b, s // tm),
        in_specs=[pl.BlockSpec((1, SUBLANES, D_MODEL), lambda bi, i: (bi, jnp.maximum(i * nb - 1, 0), 0)),
                  pl.BlockSpec((1, tm, D_MODEL), row),
                  pl.BlockSpec((1, SUBLANES, D_MODEL),
                               lambda bi, i: (bi, jnp.minimum((i + 1) * nb, s // SUBLANES - 1), 0)),
                  pl.BlockSpec((1, 1, D_MODEL), vec),
                  pl.BlockSpec((1, 1, D_MODEL), vec),
                  pl.BlockSpec((1, 1, D_MODEL), vec)]
                 + [_const_spec(w.shape) for w in weights],
        out_specs=pl.BlockSpec((1, tm, D_MODEL), row),
        out_shape=jax.ShapeDtypeStruct(x.shape, F32),
        scratch_shapes=[pltpu.VMEM((tm, D_FF), BF16)],
        compiler_params=_params("parallel", "parallel"),
        name="ffn",
    )(x, x, x, sh, sc, gt, *weights)


def _rope_tables(s):
    half = QK_ROPE // 2
    inv = 1.0 / (ROPE_BASE ** (jnp.arange(half, dtype=F32) / half))
    ang = jnp.arange(s, dtype=F32)[:, None] * inv[None, :]
    cos, sin = jnp.cos(ang), jnp.sin(ang)
    zero = jnp.zeros((s, LANES - QK_ROPE), F32)
    return (jnp.concatenate([cos, cos, zero], axis=1),
            jnp.concatenate([-sin, sin, zero], axis=1))


def _dft_cos_sin(n):
    idx = jnp.arange(n, dtype=jnp.int32)
    ang = ((idx[:, None] * idx[None, :]) % n).astype(F32) * (2.0 * math.pi / n)
    norm = n ** -0.5
    return jnp.cos(ang) * norm, jnp.sin(ang) * norm


def _fourier_tables(s):
    n1 = s // DFT_N2
    rows = BF16_SUBLANES
    new = jnp.arange(DFT_PB, dtype=jnp.int32)
    old = DFT_N2 * (new % rows) + new // rows
    perm = (old[:, None] == jnp.arange(DFT_PB, dtype=jnp.int32)[None, :]).astype(BF16)
    cc, sc_ = _dft_cos_sin(FOURIER_GROUP)
    wcs = jnp.concatenate([cc, sc_], axis=1).astype(BF16)
    j1 = jnp.arange(n1, dtype=jnp.int32)[None, :, None]
    s1 = jnp.arange(n1, dtype=jnp.int32)[None, None, :]
    s2 = jnp.arange(DFT_N2, dtype=jnp.int32)[:, None, None]
    ang = ((j1 * (DFT_N2 * s1 + s2)) % s).astype(F32) * (2.0 * math.pi / s)
    c1, sn1 = jnp.cos(ang) * s ** -0.5, jnp.sin(ang) * s ** -0.5
    w1 = jnp.concatenate([jnp.concatenate([c1, -sn1], axis=2),
                          jnp.concatenate([sn1, c1], axis=2)], axis=1).astype(BF16)
    j2 = jnp.arange(DFT_N2, dtype=jnp.int32)
    ang2 = ((j2[:, None] * j2[None, :]) % DFT_N2).astype(F32) * (2.0 * math.pi / DFT_N2)
    eye = jnp.eye(DFT_JG, dtype=F32)
    bd = jnp.concatenate([jnp.kron(jnp.cos(ang2), eye), jnp.kron(-jnp.sin(ang2), eye)], axis=1).astype(BF16)
    return perm, wcs, w1, bd


def _layer_weights(l, g_mix_pre, g_mix_post, w_in, g_q, w_q_b, g_kv, w_kv_b, w_attn_o, w_four, w_out,
                   g_ffn_pre, g_ffn_post, w_up, w_conv, b_conv, w_down):
    wi = w_in[l]
    wqb = w_q_b[l].reshape(Q_LORA, N_HEADS, QK_HEAD)
    wkvb = w_kv_b[l].reshape(KV_LORA, N_HEADS, QK_NOPE + V_HEAD)
    return {
        "g_mix_pre": g_mix_pre[l].reshape(1, D_MODEL),
        "g_mix_post": g_mix_post[l].reshape(1, D_MODEL),
        "w_qa": wi[:, :OFF_KV].astype(BF16),
        "w_kva": wi[:, OFF_KV:OFF_KR].astype(BF16),
        "w_kr": jnp.pad(wi[:, OFF_KR:OFF_F], ((0, 0), (0, LANES - QK_ROPE))).astype(BF16),
        "w_f": wi[:, OFF_F:OFF_G].astype(BF16),
        "w_g": wi[:, OFF_G:].astype(BF16),
        "g_q": g_q[l].reshape(1, Q_LORA),
        "g_kv": g_kv[l].reshape(1, KV_LORA),
        "w_qn": wqb[:, :, :QK_NOPE].reshape(Q_LORA, N_HEADS * QK_NOPE).astype(BF16),
        "w_qr": wqb[:, :, QK_NOPE:].reshape(Q_LORA, N_HEADS * QK_ROPE).astype(BF16),
        "w_kb": wkvb[:, :, :QK_NOPE].reshape(KV_LORA, N_HEADS * QK_NOPE).astype(BF16),
        "w_vbt": wkvb[:, :, QK_NOPE:].reshape(KV_LORA, N_HEADS * V_HEAD).T.astype(BF16),
        "w_attn_o": w_attn_o[l].astype(BF16),
        "w_four": w_four[l].astype(BF16),
        "w_out": w_out[l].astype(BF16),
        "g_ffn_pre": g_ffn_pre[l].reshape(1, D_MODEL),
        "g_ffn_post": g_ffn_post[l].reshape(1, D_MODEL),
        "w_up_a": w_up[l][:, :D_FF].astype(BF16),
        "w_up_b": w_up[l][:, D_FF:].astype(BF16),
        "w_conv_a": w_conv[l][:, :D_FF],
        "w_conv_b": 0.5 * w_conv[l][:, D_FF:],
        "b_conv_a": b_conv[l][:D_FF].reshape(1, D_FF),
        "b_conv_b": 0.5 * b_conv[l][D_FF:].reshape(1, D_FF),
        "w_down": w_down[l].astype(BF16),
    }


def _tile(s, want):
    t = min(s, want)
    assert s % t == 0
    return t


def _trunk(x, c, w_ada, b_ada, layers):
    b, s, _ = x.shape
    tm = _tile(s, 512)
    tm_big = _tile(s, 1024)
    tq = _tile(s, 512)
    assert s % DFT_PB == 0 and (s // DFT_N2) % DFT_JG == 0
    cos_t, sin_t = _rope_tables(s)
    four_tables = _fourier_tables(s)
    mods = _ada_mods(c, w_ada, b_ada).reshape(DEPTH, b, N_ADA, 1, D_MODEL)
    for l, lw in enumerate(layers):
        sh1, sc1, gt1, sh2, sc2, gt2 = [mods[l, :, j] for j in range(N_ADA)]
        q, k, vt, u, gates, qsq, ksq = _pre_call(x, sh1, sc1, lw, cos_t, sin_t, tm)
        four = _four_call(u, four_tables)
        attn = _attn_call(q, k, vt, qsq, ksq, tq)
        x = _merge_call(attn, four, gates, x, gt1, lw, tm_big)
        x = _ffn_call(x, sh2, sc2, gt2, lw, tm_big)
    return x


def kernel(x_prompt, x_sample, c_prompt, c_sample, w_ada, b_ada, g_mix_pre, g_mix_post, w_in, g_q, w_q_b,
           g_kv, w_kv_b, w_attn_o, w_four, w_out, g_ffn_pre, g_ffn_post, w_up, w_conv, b_conv, w_down):
    layers = [_layer_weights(l, g_mix_pre, g_mix_post, w_in, g_q, w_q_b, g_kv, w_kv_b, w_attn_o, w_four,
                             w_out, g_ffn_pre, g_ffn_post, w_up, w_conv, b_conv, w_down)
              for l in range(DEPTH)]
    y_prompt = _trunk(x_prompt, c_prompt, w_ada, b_ada, layers)
    y_sample = _trunk(x_sample, c_sample, w_ada, b_ada, layers)
    return (y_prompt, y_sample)
```

```python
import functools
import math

import jax
import jax.numpy as jnp
from jax import lax
from jax.experimental import pallas as pl
from jax.experimental.pallas import tpu as pltpu

D_MODEL = 1024
DEPTH = 4
N_HEADS = 8
N_HEAD_PAIRS = N_HEADS // 2
QK_NOPE = 128
QK_ROPE = 64
V_HEAD = 128
Q_LORA = 512
KV_LORA = 256
QK_HEAD = QK_NOPE + QK_ROPE
ROPE_BASE = 10000.0
N_FOURIER_GROUPS = 4
FOURIER_GROUP = 128
FOURIER_WIDTH = N_FOURIER_GROUPS * FOURIER_GROUP
D_FF = 2816
EPS = 1e-6
N_ADA = 6
OFF_KV = Q_LORA
OFF_KR = OFF_KV + KV_LORA
OFF_F = OFF_KR + QK_ROPE
OFF_G = OFF_F + FOURIER_WIDTH

LANES = 128
SUBLANES = 8
QK_PAD = 2 * LANES
BF16_SUBLANES = 16
V_AUG = V_HEAD
VMEM_LIMIT_BYTES = 60 * 1024 * 1024

BF16 = jnp.bfloat16
F32 = jnp.float32


def _dot(a, b):
    return jnp.dot(a, b, preferred_element_type=F32)


def _dot_nt(a, b):
    return lax.dot_general(a, b, (((1,), (1,)), ((), ())), preferred_element_type=F32)


def _rms(x, g):
    return x * lax.rsqrt(jnp.mean(x * x, axis=-1, keepdims=True) + EPS) * g


def _const_spec(shape):
    nd = len(shape)
    return pl.BlockSpec(shape, lambda *_: (0,) * nd, pipeline_mode=pl.Buffered(1))


def _params(*sem):
    return pltpu.CompilerParams(dimension_semantics=sem, vmem_limit_bytes=VMEM_LIMIT_BYTES)


def _ada_kernel(c_ref, w_ref, b_ref, o_ref):
    c = c_ref[...]
    s = c * jax.nn.sigmoid(c)
    o_ref[0] = jnp.dot(s, w_ref[0], preferred_element_type=F32,
                       precision=lax.Precision.HIGHEST) + b_ref[0]


def _ada_mods(c, w_ada, b_ada):
    b = c.shape[0]
    bp = -(-b // SUBLANES) * SUBLANES
    cp = jnp.pad(c, ((0, bp - b), (0, 0)))
    out = pl.pallas_call(
        _ada_kernel,
        grid=(DEPTH, N_ADA),
        in_specs=[
            pl.BlockSpec((bp, D_MODEL), lambda l, j: (0, 0)),
            pl.BlockSpec((1, D_MODEL, D_MODEL), lambda l, j: (l, 0, j)),
            pl.BlockSpec((1, 1, D_MODEL), lambda l, j: (l, 0, j)),
        ],
        out_specs=pl.BlockSpec((1, bp, D_MODEL), lambda l, j: (l, 0, j)),
        out_shape=jax.ShapeDtypeStruct((DEPTH, bp, N_ADA * D_MODEL), F32),
        compiler_params=_params("arbitrary", "arbitrary"),
        name="ada",
    )(cp, w_ada, b_ada.reshape(DEPTH, 1, N_ADA * D_MODEL))
    return out[:, :b]


def _pre_kernel(x_ref, sh_ref, sc_ref, gpre_ref, wqa_ref, wkva_ref, wkr_ref, wf_ref, wg_ref,
                gq_ref, gkv_ref, wqn_ref, wqr_ref, wkb_ref, wvbt_ref, hsum_ref, cos_ref, sin_ref,
                q_ref, k_ref, vt_ref, u_ref, g_ref, qsq_ref, ksq_ref):
    tm = x_ref.shape[1]
    x = x_ref[0]
    h = _rms(x, gpre_ref[...]) * (1.0 + sc_ref[0]) + sh_ref[0]
    hb = h.astype(BF16)

    zq = _dot(hb, wqa_ref[...])
    zkv = _dot(hb, wkva_ref[...])
    zkr = _dot(hb, wkr_ref[...])
    cq = _rms(zq, gq_ref[...]).astype(BF16)
    ckv = _rms(zkv, gkv_ref[...]).astype(BF16)

    cos = cos_ref[...]
    sin = sin_ref[...]
    lane = lax.broadcasted_iota(jnp.int32, (tm, LANES), 1)
    low_half = (lane % QK_ROPE) < (QK_ROPE // 2)

    def rope(t):
        rot = jnp.where(low_half, pltpu.roll(t, LANES - QK_ROPE // 2, 1), pltpu.roll(t, QK_ROPE // 2, 1))
        return t * cos + rot * sin

    scale = QK_HEAD ** -0.5 * math.log2(math.e)
    kr = rope(zkr).astype(BF16)
    qn = _dot(cq, wqn_ref[...])
    qr = _dot(cq, wqr_ref[...])
    kn = _dot(ckv, wkb_ref[...])
    u_ref[0] = _dot(hb, wf_ref[...]).astype(BF16)
    g_ref[0] = _dot(hb, wg_ref[...]).astype(BF16)
    vt_ref[0] = _dot_nt(wvbt_ref[...], ckv).astype(BF16)

    def sq(t):
        t = t.astype(F32)
        return t * t

    kr_sq = sq(kr)
    q_sq, k_sq = [], []
    for hd in range(N_HEADS):
        lo = hd * QK_PAD
        sl = slice(hd * LANES, (hd + 1) * LANES)
        q_nope = (qn[:, sl] * scale).astype(BF16)
        pair = qr[:, (hd // 2) * LANES:(hd // 2 + 1) * LANES]
        mine = pair if hd % 2 == 0 else pltpu.roll(pair, QK_ROPE, 1)
        q_rope = (rope(jnp.where(lane < QK_ROPE, mine, 0.0)) * scale).astype(BF16)
        k_nope = kn[:, sl].astype(BF16)
        q_ref[0, :, lo:lo + LANES] = q_nope
        q_ref[0, :, lo + LANES:lo + QK_PAD] = q_rope
        k_ref[0, :, lo:lo + LANES] = k_nope
        k_ref[0, :, lo + LANES:lo + QK_PAD] = kr
        q_sq.append(sq(q_nope) + sq(q_rope))
        k_sq.append(sq(k_nope) + kr_sq)

    def pair_sums(parts):
        return jnp.concatenate([(parts[2 * p] + parts[2 * p + 1]).astype(BF16) for p in range(N_HEAD_PAIRS)], axis=1)

    qsq_ref[0] = _dot_nt(hsum_ref[...], pair_sums(q_sq))[:N_HEAD_PAIRS]
    ksq_ref[0] = _dot_nt(hsum_ref[...], pair_sums(k_sq))[:N_HEAD_PAIRS]


def _head_sum_matrix():
    r = jnp.arange(BF16_SUBLANES, dtype=jnp.int32)[:, None]
    c = jnp.arange(N_HEAD_PAIRS * LANES, dtype=jnp.int32)[None, :] // LANES
    return (r == c).astype(BF16)


def _pre_call(x, sh, sc, lw, cos_t, sin_t, tm):
    b, s, _ = x.shape
    row = lambda bi, i: (bi, i, 0)
    vec = lambda bi, i: (bi, 0, 0)
    weights = [lw["g_mix_pre"], lw["w_qa"], lw["w_kva"], lw["w_kr"], lw["w_f"], lw["w_g"],
               lw["g_q"], lw["g_kv"], lw["w_qn"], lw["w_qr"], lw["w_kb"], lw["w_vbt"], _head_sum_matrix()]
    return pl.pallas_call(
        _pre_kernel,
        grid=(b, s // tm),
        in_specs=[pl.BlockSpec((1, tm, D_MODEL), row),
                  pl.BlockSpec((1, 1, D_MODEL), vec),
                  pl.BlockSpec((1, 1, D_MODEL), vec)]
                 + [_const_spec(w.shape) for w in weights]
                 + [pl.BlockSpec((tm, LANES), lambda bi, i: (i, 0)),
                    pl.BlockSpec((tm, LANES), lambda bi, i: (i, 0))],
        out_specs=[pl.BlockSpec((1, tm, N_HEADS * QK_PAD), row),
                   pl.BlockSpec((1, tm, N_HEADS * QK_PAD), row),
                   pl.BlockSpec((1, N_HEADS * V_AUG, tm), lambda bi, i: (bi, 0, i)),
                   pl.BlockSpec((1, tm, FOURIER_WIDTH), row),
                   pl.BlockSpec((1, tm, 2 * D_MODEL), row),
                   pl.BlockSpec((1, N_HEAD_PAIRS, tm), lambda bi, i: (bi, 0, i)),
                   pl.BlockSpec((1, N_HEAD_PAIRS, tm), lambda bi, i: (bi, 0, i))],
        out_shape=[jax.ShapeDtypeStruct((b, s, N_HEADS * QK_PAD), BF16),
                   jax.ShapeDtypeStruct((b, s, N_HEADS * QK_PAD), BF16),
                   jax.ShapeDtypeStruct((b, N_HEADS * V_AUG, s), BF16),
                   jax.ShapeDtypeStruct((b, s, FOURIER_WIDTH), BF16),
                   jax.ShapeDtypeStruct((b, s, 2 * D_MODEL), BF16),
                   jax.ShapeDtypeStruct((b, N_HEAD_PAIRS, s), F32),
                   jax.ShapeDtypeStruct((b, N_HEAD_PAIRS, s), F32)],
        compiler_params=_params("parallel", "parallel"),
        name="pre",
    )(x, sh, sc, *weights, cos_t, sin_t)


DFT_N2 = 16
DFT_JG = BF16_SUBLANES
DFT_PB = DFT_N2 * BF16_SUBLANES


def _four_kernel(u_ref, perm_ref, wcs_ref, w1_ref, bd_ref, o_ref, ab_ref, z_ref):
    s = u_ref.shape[1]
    n1 = s // DFT_N2
    rows = BF16_SUBLANES
    for blk in range(s // DFT_PB):
        ub = u_ref[0, blk * DFT_PB:(blk + 1) * DFT_PB, :]
        up = _dot(perm_ref[...], ub).astype(BF16)
        for g in range(N_FOURIER_GROUPS):
            sl = slice(g * FOURIER_GROUP, (g + 1) * FOURIER_GROUP)
            ab = _dot(up[:, sl], wcs_ref[...]).astype(BF16)
            for s2 in range(DFT_N2):
                src = slice(s2 * rows, (s2 + 1) * rows)
                dst = slice(blk * rows, (blk + 1) * rows)
                ab_ref[s2, dst, sl] = ab[src, :FOURIER_GROUP]
                ab_ref[s2, n1 + blk * rows:n1 + (blk + 1) * rows, sl] = ab[src, FOURIER_GROUP:]
    kg = 2 * DFT_N2 * DFT_JG
    for s2 in range(DFT_N2):
        z = _dot(w1_ref[s2], ab_ref[s2]).astype(BF16)
        for grp in range(n1 // DFT_JG):
            for r in range(2):
                dst = grp * kg + r * DFT_N2 * DFT_JG + s2 * DFT_JG
                z_ref[dst:dst + DFT_JG, :] = z[r * n1 + grp * DFT_JG:r * n1 + (grp + 1) * DFT_JG, :]
    for grp in range(n1 // DFT_JG):
        y = _dot(bd_ref[...], z_ref[grp * kg:(grp + 1) * kg, :]).astype(BF16)
        for j2 in range(DFT_N2):
            dst = n1 * j2 + grp * DFT_JG
            o_ref[0, dst:dst + DFT_JG, :] = y[j2 * DFT_JG:(j2 + 1) * DFT_JG, :]


def _four_call(u, tables):
    b, s, _ = u.shape
    n1 = s // DFT_N2
    return pl.pallas_call(
        _four_kernel,
        grid=(b,),
        in_specs=[pl.BlockSpec((1, s, FOURIER_WIDTH), lambda bi: (bi, 0, 0))]
                 + [_const_spec(t.shape) for t in tables],
        out_specs=pl.BlockSpec((1, s, FOURIER_WIDTH), lambda bi: (bi, 0, 0)),
        out_shape=jax.ShapeDtypeStruct((b, s, FOURIER_WIDTH), BF16),
        scratch_shapes=[pltpu.VMEM((DFT_N2, 2 * n1, FOURIER_WIDTH), BF16),
                        pltpu.VMEM((2 * s, FOURIER_WIDTH), BF16)],
        compiler_params=_params("parallel"),
        name="four",
    )(u, *tables)


ATTN_MIN_SUM = 2.0 ** -64
ATTN_BOUND_SLACK = 1.0 + 2.0 ** -6


ATTN_STEP_KEY_ROWS = 4096


def _attn_kernel(q_ref, k_ref, vt_ref, qsq_ref, ksq_ref, o_ref, *, tq, heads):
    nq = q_ref.shape[1] // tq

    def tile(hh, cols, m_of):
        qk = slice(hh * QK_PAD, (hh + 1) * QK_PAD)
        st = _dot_nt(k_ref[0, :, qk], q_ref[0, cols, qk])
        p = jnp.exp2(st - m_of(st))
        l = jnp.sum(p, axis=0, keepdims=True)
        ot = _dot(vt_ref[0, hh * V_AUG:(hh + 1) * V_AUG, :], p.astype(BF16))
        o_ref[0, cols, hh * V_HEAD:(hh + 1) * V_HEAD] = (ot * (1.0 / l)).T.astype(BF16)
        return l

    l_min = None
    k_max = jnp.sqrt(jnp.max(ksq_ref[0, 0], axis=-1, keepdims=True)) * ATTN_BOUND_SLACK
    for hh in range(heads):
        for j in range(nq):
            cols = slice(j * tq, (j + 1) * tq)
            bound = jnp.sqrt(qsq_ref[0, 0, :, cols]) * k_max
            l = tile(hh, cols, lambda st: bound)
            l_min = l if l_min is None else jnp.minimum(l_min, l)

    @pl.when(jnp.logical_not(jnp.min(l_min) >= ATTN_MIN_SUM))
    def _():
        for hh in range(heads):
            def body(j, carry):
                cols = pl.ds(pl.multiple_of(j * tq, tq), tq)
                tile(hh, cols, lambda st: jnp.max(st, axis=0, keepdims=True))
                return carry
            lax.fori_loop(0, nq, body, 0)


def _attn_call(q, k, vt, qsq, ksq, tq):
    b, s, _ = q.shape
    hps = max(1, min(2, ATTN_STEP_KEY_ROWS // s))
    norm_spec = pl.BlockSpec((1, 1, 1, s), lambda bi, h: (bi, (h * hps) // 2, 0, 0))
    return pl.pallas_call(
        functools.partial(_attn_kernel, tq=tq, heads=hps),
        grid=(b, N_HEADS // hps),
        in_specs=[pl.BlockSpec((1, s, hps * QK_PAD), lambda bi, h: (bi, 0, h)),
                  pl.BlockSpec((1, s, hps * QK_PAD), lambda bi, h: (bi, 0, h)),
                  pl.BlockSpec((1, hps * V_AUG, s), lambda bi, h: (bi, h, 0)),
                  norm_spec, norm_spec],
        out_specs=pl.BlockSpec((1, s, hps * V_HEAD), lambda bi, h: (bi, 0, h)),
        out_shape=jax.ShapeDtypeStruct((b, s, N_HEADS * V_HEAD), BF16),
        compiler_params=_params("parallel", "parallel"),
        name="attn",
    )(q, k, vt, qsq.reshape(b, N_HEAD_PAIRS, 1, s), ksq.reshape(b, N_HEAD_PAIRS, 1, s))


def _merge_kernel(a_ref, f_ref, g_ref, x_ref, gt_ref, gpost_ref, wao_ref, wfo_ref, wout_ref, o_ref):
    half = x_ref.shape[1] // 2
    halves = (slice(0, half), slice(half, 2 * half))
    branches = [(_dot(a_ref[0, rows, :], wao_ref[...]), _dot(f_ref[0, rows, :], wfo_ref[...])) for rows in halves]
    ys = []
    for rows, (branch_a, branch_b) in zip(halves, branches):
        gate_a = g_ref[0, rows, :D_MODEL].astype(F32)
        gate_b = g_ref[0, rows, D_MODEL:].astype(F32)
        merged = jax.nn.sigmoid(gate_a) * branch_a + jax.nn.sigmoid(gate_b) * branch_b
        ys.append(_dot(merged.astype(BF16), wout_ref[...]))
    for rows, y in zip(halves, ys):
        o_ref[0, rows, :] = x_ref[0, rows, :] + gt_ref[0] * _rms(y, gpost_ref[...])


def _merge_call(attn, four, gates, x, gt, lw, tm):
    b, s, _ = x.shape
    row = lambda bi, i: (bi, i, 0)
    weights = [lw["g_mix_post"], lw["w_attn_o"], lw["w_four"], lw["w_out"]]
    return pl.pallas_call(
        _merge_kernel,
        grid=(b, s // tm),
        in_specs=[pl.BlockSpec((1, tm, N_HEADS * V_HEAD), row),
                  pl.BlockSpec((1, tm, FOURIER_WIDTH), row),
                  pl.BlockSpec((1, tm, 2 * D_MODEL), row),
                  pl.BlockSpec((1, tm, D_MODEL), row),
                  pl.BlockSpec((1, 1, D_MODEL), lambda bi, i: (bi, 0, 0))]
                 + [_const_spec(w.shape) for w in weights],
        out_specs=pl.BlockSpec((1, tm, D_MODEL), row),
        out_shape=jax.ShapeDtypeStruct(x.shape, F32),
        compiler_params=_params("parallel", "parallel"),
        name="merge",
    )(attn, four, gates, x, gt, *weights)


FFN_CHUNK = 2 * LANES
assert D_FF % FFN_CHUNK == 0
GELU_C = math.sqrt(2.0 / math.pi)


def _gelu_tanh_x2(a):
    return a * (1.0 + jnp.tanh(a * (GELU_C + (GELU_C * 0.044715) * (a * a))))


def _ffn_kernel(xp_ref, x_ref, xn_ref, sh_ref, sc_ref, gt_ref, gpre_ref, gpost_ref,
                wua_ref, wub_ref, wca_ref, wcb_ref, bca_ref, bcb_ref, wd_ref, o_ref, act_ref):
    i = pl.program_id(1)
    last = pl.num_programs(1) - 1
    tm = x_ref.shape[1]
    te = tm + 2 * SUBLANES
    x = x_ref[0]
    xe = jnp.concatenate([xp_ref[0], x, xn_ref[0]], axis=0)
    he = _rms(xe, gpre_ref[...]) * (1.0 + sc_ref[0]) + sh_ref[0]
    r = lax.broadcasted_iota(jnp.int32, (te, 1), 0)
    inside = jnp.logical_and(jnp.logical_or(r >= SUBLANES, i > 0),
                             jnp.logical_or(r < tm + SUBLANES, i < last))
    he = jnp.where(inside, he, 0.0).astype(BF16)

    def conv(u, w_ref, b_ref, cs):
        prev = pltpu.roll(u, 1, 0)[SUBLANES:SUBLANES + tm]
        nxt = pltpu.roll(u, te - 1, 0)[SUBLANES:SUBLANES + tm]
        mid = u[SUBLANES:SUBLANES + tm]
        return prev * w_ref[0:1, cs] + mid * w_ref[1:2, cs] + nxt * w_ref[2:3, cs] + b_ref[:, cs]

    for c in range(D_FF // FFN_CHUNK):
        cs = slice(c * FFN_CHUNK, (c + 1) * FFN_CHUNK)
        a = conv(_dot(he, wua_ref[:, cs]), wca_ref, bca_ref, cs)
        g = conv(_dot(he, wub_ref[:, cs]), wcb_ref, bcb_ref, cs)
        act_ref[:, cs] = (_gelu_tanh_x2(a) * g).astype(BF16)

    y = _dot(act_ref[...], wd_ref[...])
    o_ref[0] = x + gt_ref[0] * _rms(y, gpost_ref[...])


def _ffn_call(x, sh, sc, gt, lw, tm):
    b, s, _ = x.shape
    nb = tm // SUBLANES
    row = lambda bi, i: (bi, i, 0)
    vec = lambda bi, i: (bi, 0, 0)
    weights = [lw["g_ffn_pre"], lw["g_ffn_post"], lw["w_up_a"], lw["w_up_b"],
               lw["w_conv_a"], lw["w_conv_b"], lw["b_conv_a"], lw["b_conv_b"], lw["w_down"]]
    return pl.pallas_call(
        _ffn_kernel,
        grid=(b, s // tm),
        in_specs=[pl.BlockSpec((1, SUBLANES, D_MODEL), lambda bi, i: (bi, jnp.maximum(i * nb - 1, 0), 0)),
                  pl.BlockSpec((1, tm, D_MODEL), row),
                  pl.BlockSpec((1, SUBLANES, D_MODEL),
                               lambda bi, i: (bi, jnp.minimum((i + 1) * nb, s // SUBLANES - 1), 0)),
                  pl.BlockSpec((1, 1, D_MODEL), vec),
                  pl.BlockSpec((1, 1, D_MODEL), vec),
                  pl.BlockSpec((1, 1, D_MODEL), vec)]
                 + [_const_spec(w.shape) for w in weights],
        out_specs=pl.BlockSpec((1, tm, D_MODEL), row),
        out_shape=jax.ShapeDtypeStruct(x.shape, F32),
        scratch_shapes=[pltpu.VMEM((tm, D_FF), BF16)],
        compiler_params=_params("parallel", "parallel"),
        name="ffn",
    )(x, x, x, sh, sc, gt, *weights)


def _rope_tables(s):
    half = QK_ROPE // 2
    inv = 1.0 / (ROPE_BASE ** (jnp.arange(half, dtype=F32) / half))
    ang = jnp.arange(s, dtype=F32)[:, None] * inv[None, :]
    cos, sin = jnp.cos(ang), jnp.sin(ang)
    zero = jnp.zeros((s, LANES - QK_ROPE), F32)
    return (jnp.concatenate([cos, cos, zero], axis=1),
            jnp.concatenate([-sin, sin, zero], axis=1))


def _dft_cos_sin(n):
    idx = jnp.arange(n, dtype=jnp.int32)
    ang = ((idx[:, None] * idx[None, :]) % n).astype(F32) * (2.0 * math.pi / n)
    norm = n ** -0.5
    return jnp.cos(ang) * norm, jnp.sin(ang) * norm


def _fourier_tables(s):
    n1 = s // DFT_N2
    rows = BF16_SUBLANES
    new = jnp.arange(DFT_PB, dtype=jnp.int32)
    old = DFT_N2 * (new % rows) + new // rows
    perm = (old[:, None] == jnp.arange(DFT_PB, dtype=jnp.int32)[None, :]).astype(BF16)
    cc, sc_ = _dft_cos_sin(FOURIER_GROUP)
    wcs = jnp.concatenate([cc, sc_], axis=1).astype(BF16)
    j1 = jnp.arange(n1, dtype=jnp.int32)[None, :, None]
    s1 = jnp.arange(n1, dtype=jnp.int32)[None, None, :]
    s2 = jnp.arange(DFT_N2, dtype=jnp.int32)[:, None, None]
    ang = ((j1 * (DFT_N2 * s1 + s2)) % s).astype(F32) * (2.0 * math.pi / s)
    c1, sn1 = jnp.cos(ang) * s ** -0.5, jnp.sin(ang) * s ** -0.5
    w1 = jnp.concatenate([jnp.concatenate([c1, -sn1], axis=2),
                          jnp.concatenate([sn1, c1], axis=2)], axis=1).astype(BF16)
    j2 = jnp.arange(DFT_N2, dtype=jnp.int32)
    ang2 = ((j2[:, None] * j2[None, :]) % DFT_N2).astype(F32) * (2.0 * math.pi / DFT_N2)
    eye = jnp.eye(DFT_JG, dtype=F32)
    bd = jnp.concatenate([jnp.kron(jnp.cos(ang2), eye), jnp.kron(-jnp.sin(ang2), eye)], axis=1).astype(BF16)
    return perm, wcs, w1, bd


def _layer_weights(l, g_mix_pre, g_mix_post, w_in, g_q, w_q_b, g_kv, w_kv_b, w_attn_o, w_four, w_out,
                   g_ffn_pre, g_ffn_post, w_up, w_conv, b_conv, w_down):
    wi = w_in[l]
    wqb = w_q_b[l].reshape(Q_LORA, N_HEADS, QK_HEAD)
    wkvb = w_kv_b[l].reshape(KV_LORA, N_HEADS, QK_NOPE + V_HEAD)
    return {
        "g_mix_pre": g_mix_pre[l].reshape(1, D_MODEL),
        "g_mix_post": g_mix_post[l].reshape(1, D_MODEL),
        "w_qa": wi[:, :OFF_KV].astype(BF16),
        "w_kva": wi[:, OFF_KV:OFF_KR].astype(BF16),
        "w_kr": jnp.pad(wi[:, OFF_KR:OFF_F], ((0, 0), (0, LANES - QK_ROPE))).astype(BF16),
        "w_f": wi[:, OFF_F:OFF_G].astype(BF16),
        "w_g": wi[:, OFF_G:].astype(BF16),
        "g_q": g_q[l].reshape(1, Q_LORA),
        "g_kv": g_kv[l].reshape(1, KV_LORA),
        "w_qn": wqb[:, :, :QK_NOPE].reshape(Q_LORA, N_HEADS * QK_NOPE).astype(BF16),
        "w_qr": wqb[:, :, QK_NOPE:].reshape(Q_LORA, N_HEADS * QK_ROPE).astype(BF16),
        "w_kb": wkvb[:, :, :QK_NOPE].reshape(KV_LORA, N_HEADS * QK_NOPE).astype(BF16),
        "w_vbt": wkvb[:, :, QK_NOPE:].reshape(KV_LORA, N_HEADS * V_HEAD).T.astype(BF16),
        "w_attn_o": w_attn_o[l].astype(BF16),
        "w_four": w_four[l].astype(BF16),
        "w_out": w_out[l].astype(BF16),
        "g_ffn_pre": g_ffn_pre[l].reshape(1, D_MODEL),
        "g_ffn_post": g_ffn_post[l].reshape(1, D_MODEL),
        "w_up_a": w_up[l][:, :D_FF].astype(BF16),
        "w_up_b": w_up[l][:, D_FF:].astype(BF16),
        "w_conv_a": w_conv[l][:, :D_FF],
        "w_conv_b": 0.5 * w_conv[l][:, D_FF:],
        "b_conv_a": b_conv[l][:D_FF].reshape(1, D_FF),
        "b_conv_b": 0.5 * b_conv[l][D_FF:].reshape(1, D_FF),
        "w_down": w_down[l].astype(BF16),
    }


def _tile(s, want):
    t = min(s, want)
    assert s % t == 0
    return t


def _trunk(x, c, w_ada, b_ada, layers):
    b, s, _ = x.shape
    tm = _tile(s, 512)
    tm_big = _tile(s, 1024)
    tq = _tile(s, 512)
    assert s % DFT_PB == 0 and (s // DFT_N2) % DFT_JG == 0
    cos_t, sin_t = _rope_tables(s)
    four_tables = _fourier_tables(s)
    mods = _ada_mods(c, w_ada, b_ada).reshape(DEPTH, b, N_ADA, 1, D_MODEL)
    for l, lw in enumerate(layers):
        sh1, sc1, gt1, sh2, sc2, gt2 = [mods[l, :, j] for j in range(N_ADA)]
        q, k, vt, u, gates, qsq, ksq = _pre_call(x, sh1, sc1, lw, cos_t, sin_t, tm)
        four = _four_call(u, four_tables)
        attn = _attn_call(q, k, vt, qsq, ksq, tq)
        x = _merge_call(attn, four, gates, x, gt1, lw, tm_big)
        x = _ffn_call(x, sh2, sc2, gt2, lw, tm_big)
    return x


def kernel(x_prompt, x_sample, c_prompt, c_sample, w_ada, b_ada, g_mix_pre, g_mix_post, w_in, g_q, w_q_b,
           g_kv, w_kv_b, w_attn_o, w_four, w_out, g_ffn_pre, g_ffn_post, w_up, w_conv, b_conv, w_down):
    layers = [_layer_weights(l, g_mix_pre, g_mix_post, w_in, g_q, w_q_b, g_kv, w_kv_b, w_attn_o, w_four,
                             w_out, g_ffn_pre, g_ffn_post, w_up, w_conv, b_conv, w_down)
              for l in range(DEPTH)]
    y_prompt = _trunk(x_prompt, c_prompt, w_ada, b_ada, layers)
    y_sample = _trunk(x_sample, c_sample, w_ada, b_ada, layers)
    return (y_prompt, y_sample)
```
